```python
import jax, jax.numpy as jnp
from jax import lax
import numpy as np

D_MODEL = 1024
BATCH = 16
SEQ = 4096
DEPTH = 4
DEC_BATCH = 8
DEC_SEQ = 4096
PAST_LEN = 128

N_MIXERS = 2
RMS_EPS = 1e-6
NEG_BIG = -1e30

A_GROUPS = ((128, 1), (512, 4), (2048, 16))
A_HEADS = 8
A_HEAD_DIM = 128
A_GROUP_WIDTH = A_HEADS * A_HEAD_DIM
A_BRANCH = A_GROUP_WIDTH
A_IN = len(A_GROUPS) * 3 * A_GROUP_WIDTH + A_BRANCH
ALIBI_MAX_EXP = 8.0

B_HEADS = 16
B_NOPE = 64
B_ROPE = 32
B_V = 64
B_Q_LORA = 384
B_KV_LORA = 256
B_BRANCH = B_HEADS * B_V
B_IN = B_Q_LORA + B_KV_LORA + B_ROPE + B_BRANCH
ROPE_THETA = 10000.0
Q_BLOCK = 128

kernel_name = "hybrid_dilated_mla_encoder"


def rms_norm(x, g):
    xf = x.astype(jnp.float32)
    y = xf * lax.rsqrt(jnp.mean(xf * xf, axis=-1, keepdims=True) + RMS_EPS)
    return (y * g.astype(jnp.float32)).astype(x.dtype)


def alibi_slopes(n):
    return (2.0 ** (-ALIBI_MAX_EXP * np.arange(1, n + 1) / n)).astype(np.float32)


def dilated_window_attention(q, k, v, slopes, window, dilation):
    B, S, H, dh = q.shape
    radius = (window // 2) // dilation
    blk = radius
    L = S // dilation
    nb = -(-L // blk)
    Lp = nb * blk

    def to_sub(t, pad_lo, pad_hi):
        t = t.reshape(B, L, dilation, H, dh).transpose(0, 2, 3, 1, 4)
        t = jnp.pad(t, ((0, 0), (0, 0), (0, 0), (pad_lo, pad_hi), (0, 0)))
        return t.reshape(B, dilation, H, -1, blk, dh)

    qb = to_sub(q, 0, Lp - L)
    kb = to_sub(k, blk, Lp - L + blk)
    vb = to_sub(v, blk, Lp - L + blk)
    scale = dh ** -0.5
    s = jnp.concatenate(
        [jnp.einsum('brhnqd,brhnkd->brhnqk', qb, kb[:, :, :, c:c + nb]) for c in range(3)],
        axis=-1).astype(jnp.float32) * scale

    i = np.arange(blk)[:, None]
    j = np.arange(3 * blk)[None, :]
    delta = j - blk - i
    u_q = np.arange(nb)[:, None, None] * blk + i[None]
    u_k = u_q + delta[None]
    valid = (np.abs(delta) <= radius)[None] & (u_k >= 0) & (u_k < L)
    bias = (-slopes[:, None, None] * (np.abs(delta) * dilation)[None]).astype(np.float32)
    s = s + jnp.asarray(bias)[:, None]
    s = jnp.where(jnp.asarray(valid), s, NEG_BIG)
    lse = jax.nn.logsumexp(s, axis=-1)
    p = jnp.exp(s - lse[..., None]).astype(v.dtype)
    o = sum(jnp.einsum('brhnqk,brhnkd->brhnqd', p[..., c * blk:(c + 1) * blk],
                       vb[:, :, :, c:c + nb]) for c in range(3))

    def from_sub(t):
        t = t.reshape(B, dilation, H, Lp, -1)[:, :, :, :L]
        return t.transpose(0, 3, 1, 2, 4).reshape(B, S, H, -1)

    return from_sub(o), from_sub(lse[..., None])[..., 0]


def dilated_mixer(h, w_in, w_out):
    B, S, _ = h.shape
    proj = h @ w_in
    gw = A_GROUP_WIDTH
    slopes = alibi_slopes(len(A_GROUPS) * A_HEADS).reshape(len(A_GROUPS), A_HEADS)
    outs, lses = [], []
    for g, (window, dil) in enumerate(A_GROUPS):
        base = 3 * gw * g
        q, k, v = (proj[..., base + n * gw: base + (n + 1) * gw].reshape(B, S, A_HEADS, A_HEAD_DIM)
                   for n in range(3))
        o, lse = dilated_window_attention(q, k, v, slopes[g], window, dil)
        outs.append(o)
        lses.append(lse)
    wts = jax.nn.softmax(jnp.stack(lses), axis=0)
    o = jnp.sum(wts[..., None] * jnp.stack(outs).astype(jnp.float32), axis=0)
    gate = proj[..., -A_BRANCH:]
    y = o.reshape(B, S, A_BRANCH).astype(h.dtype) * jax.nn.silu(gate)
    return y @ w_out


def rope_tables(S):
    inv_freq = (ROPE_THETA ** (-np.arange(0, B_ROPE, 2) / B_ROPE)).astype(np.float32)
    ang = jnp.arange(S, dtype=jnp.float32)[:, None] * jnp.asarray(inv_freq)[None, :]
    return jnp.cos(ang), jnp.sin(ang)


def apply_rope(x, cos, sin):
    xf = x.astype(jnp.float32)
    half = xf.shape[-1] // 2
    x1, x2 = xf[..., :half], xf[..., half:]
    c, s = cos[None, :, None, :], sin[None, :, None, :]
    return jnp.concatenate([x1 * c - x2 * s, x1 * s + x2 * c], axis=-1).astype(x.dtype)


def blockwise_attention(q, k, v):
    B, S, H, dq = q.shape
    scale = dq ** -0.5
    qb = q.reshape(B, S // Q_BLOCK, Q_BLOCK, H, dq).transpose(1, 0, 2, 3, 4)

    def one_block(qblk):
        s = jnp.einsum('bqhd,bkhd->bhqk', qblk, k).astype(jnp.float32) * scale
        p = jax.nn.softmax(s, axis=-1).astype(v.dtype)
        return jnp.einsum('bhqk,bkhd->bqhd', p, v)

    o = lax.map(one_block, qb)
    return o.transpose(1, 0, 2, 3, 4).reshape(B, S, H, -1)


def mla_mixer(h, w_in, q_norm_g, w_q_up, kv_norm_g, w_kv_up, w_out):
    B, S, _ = h.shape
    proj = h @ w_in
    c_q, c_kv, k_rope, gate = jnp.split(
        proj, [B_Q_LORA, B_Q_LORA + B_KV_LORA, B_Q_LORA + B_KV_LORA + B_ROPE], axis=-1)
    q = (rms_norm(c_q, q_norm_g) @ w_q_up).reshape(B, S, B_HEADS, B_NOPE + B_ROPE)
    kv = (rms_norm(c_kv, kv_norm_g) @ w_kv_up).reshape(B, S, B_HEADS, B_NOPE + B_V)
    cos, sin = rope_tables(S)
    q_rope = apply_rope(q[..., B_NOPE:], cos, sin)
    k_rope = apply_rope(k_rope[:, :, None, :], cos, sin)
    qf = jnp.concatenate([q[..., :B_NOPE], q_rope], axis=-1)
    kf = jnp.concatenate([kv[..., :B_NOPE], jnp.broadcast_to(k_rope, (B, S, B_HEADS, B_ROPE))], axis=-1)
    o = blockwise_attention(qf, kf, kv[..., B_NOPE:])
    y = o.reshape(B, S, B_BRANCH) * jax.nn.silu(gate)
    return y @ w_out


def encoder_trunk(x, norm_g, a_w_in, a_w_out, b_w_in, b_q_norm_g, b_w_q_up,
                  b_kv_norm_g, b_w_kv_up, b_w_out, final_norm_g):
    for i in range(DEPTH):
        h = rms_norm(x, norm_g[i])
        j = i // N_MIXERS
        if i % N_MIXERS == 0:
            y = dilated_mixer(h, a_w_in[j], a_w_out[j])
        else:
            y = mla_mixer(h, b_w_in[j], b_q_norm_g[j], b_w_q_up[j],
                          b_kv_norm_g[j], b_w_kv_up[j], b_w_out[j])
        x = x + y
    return rms_norm(x, final_norm_g)


def setup_inputs(seed: int = 0) -> dict:
    key = jax.random.key(seed)
    ks = jax.random.split(key, 13)
    n_a = (DEPTH + 1) // 2
    n_b = DEPTH // 2
    f32 = jnp.float32

    def nrm(k, shape, scale):
        return jax.random.normal(k, shape, f32) * scale

    return {
        "x_prompt": nrm(ks[0], (BATCH, SEQ, D_MODEL), 1.0),
        "x_sample": nrm(ks[1], (DEC_BATCH, DEC_SEQ, D_MODEL), 1.0),
        "norm_g": 1.0 + nrm(ks[2], (DEPTH, D_MODEL), 0.01),
        "a_w_in": nrm(ks[3], (n_a, D_MODEL, A_IN), D_MODEL ** -0.5),
        "a_w_out": nrm(ks[4], (n_a, A_BRANCH, D_MODEL), A_BRANCH ** -0.5),
        "b_w_in": nrm(ks[5], (n_b, D_MODEL, B_IN), D_MODEL ** -0.5),
        "b_q_norm_g": 1.0 + nrm(ks[6], (n_b, B_Q_LORA), 0.01),
        "b_w_q_up": nrm(ks[7], (n_b, B_Q_LORA, B_HEADS * (B_NOPE + B_ROPE)), B_Q_LORA ** -0.5),
        "b_kv_norm_g": 1.0 + nrm(ks[8], (n_b, B_KV_LORA), 0.01),
        "b_w_kv_up": nrm(ks[9], (n_b, B_KV_LORA, B_HEADS * (B_NOPE + B_V)), B_KV_LORA ** -0.5),
        "b_w_out": nrm(ks[10], (n_b, B_BRANCH, D_MODEL), B_BRANCH ** -0.5),
        "final_norm_g": 1.0 + nrm(ks[11], (D_MODEL,), 0.01),
    }


def reference(x_prompt, x_sample, norm_g, a_w_in, a_w_out, b_w_in, b_q_norm_g,
              b_w_q_up, b_kv_norm_g, b_w_kv_up, b_w_out, final_norm_g):
    y_prompt = encoder_trunk(x_prompt, norm_g, a_w_in, a_w_out, b_w_in, b_q_norm_g,
                             b_w_q_up, b_kv_norm_g, b_w_kv_up, b_w_out, final_norm_g)
    y_sample = encoder_trunk(x_sample, norm_g, a_w_in, a_w_out, b_w_in, b_q_norm_g,
                             b_w_q_up, b_kv_norm_g, b_w_kv_up, b_w_out, final_norm_g)
    return (y_prompt, y_sample)
```

```python
import functools

import numpy as np
import jax
import jax.numpy as jnp
from jax import lax
from jax.experimental import pallas as pl
from jax.experimental.pallas import tpu as pltpu

F32 = jnp.float32
BF16 = jnp.bfloat16

D_MODEL = 1024
DEPTH = 4
RMS_EPS = 1e-6
NEG_BIG = -1e30

A_GROUPS = ((128, 1), (512, 4), (2048, 16))
A_HEADS = 8
A_HEAD_DIM = 128
A_GROUP_WIDTH = A_HEADS * A_HEAD_DIM
A_NQKV = 3 * len(A_GROUPS)
ALIBI_MAX_EXP = 8.0
A_RADIUS = 64
A_QSUB = 128
A_KWIN = A_QSUB + 2 * A_RADIUS

B_HEADS = 16
B_NOPE = 64
B_ROPE = 32
B_V = 64
B_Q_LORA = 384
B_KV_LORA = 256
B_QK_PAD = 128
ROPE_THETA = 10000.0

VMEM_LIMIT = 48 * 1024 * 1024


def _params(*sem):
    return pltpu.CompilerParams(dimension_semantics=sem, vmem_limit_bytes=VMEM_LIMIT)


def _rms(xf, g):
    ms = jnp.mean(xf * xf, axis=-1, keepdims=True)
    return xf * lax.rsqrt(ms + RMS_EPS) * g


def _silu(g):
    return g * (1.0 / (1.0 + jnp.exp(-g)))


def _dot_nt(a, b):
    return lax.dot_general(a, b, (((1,), (1,)), ((), ())), preferred_element_type=F32)


def _a_proj_kernel(x_ref, g_ref, w_ref, qkv_ref, gate_ref, h_scr):
    j = pl.program_id(1)

    @pl.when(j == 0)
    def _():
        h_scr[...] = _rms(x_ref[...], g_ref[...]).astype(BF16)

    acc = jnp.dot(h_scr[...], w_ref[...], preferred_element_type=F32)

    @pl.when(j < A_NQKV)
    def _():
        qkv_ref[...] = acc.astype(BF16)

    @pl.when(j == A_NQKV)
    def _():
        gate_ref[...] = acc


def _a_proj(x2, g, w_bf, tm=1024):
    t = x2.shape[0]
    wcol = A_GROUP_WIDTH
    return pl.pallas_call(
        _a_proj_kernel,
        grid=(t // tm, A_NQKV + 1),
        in_specs=[
            pl.BlockSpec((tm, D_MODEL), lambda i, j: (i, 0)),
            pl.BlockSpec((1, D_MODEL), lambda i, j: (0, 0)),
            pl.BlockSpec((D_MODEL, wcol), lambda i, j: (0, j)),
        ],
        out_specs=[
            pl.BlockSpec((tm, wcol), lambda i, j: (i, jnp.minimum(j, A_NQKV - 1))),
            pl.BlockSpec((tm, wcol), lambda i, j: (i, 0)),
        ],
        out_shape=[
            jax.ShapeDtypeStruct((t, A_NQKV * wcol), BF16),
            jax.ShapeDtypeStruct((t, wcol), F32),
        ],
        scratch_shapes=[pltpu.VMEM((tm, D_MODEL), BF16)],
        compiler_params=_params("parallel", "arbitrary"),
        name="a_proj",
    )(x2, g, w_bf)


def _win_attn_kernel(*refs, qs, seq_l, has_prev, is_last):
    it = iter(refs)
    q_ref = next(it)
    kp_ref, km_ref, kn_ref = next(it), next(it), next(it)
    vp_ref, vm_ref, vn_ref = next(it), next(it), next(it)
    bias_ref = next(it)
    if has_prev:
        oprev_ref, lprev_ref = next(it), next(it)
    if is_last:
        gate_ref, x_ref, wout_ref = next(it), next(it), next(it)
        xnew_ref = next(it)
    else:
        o_ref, lse_ref = next(it), next(it)
    kwin, vwin = next(it), next(it)
    if is_last:
        y_scr = next(it)

    i = pl.program_id(2)
    r = A_RADIUS
    kwin[0:r] = kp_ref[0]
    kwin[r:r + qs] = km_ref[0]
    kwin[r + qs:] = kn_ref[0]
    vwin[0:r] = vp_ref[0]
    vwin[r:r + qs] = vm_ref[0]
    vwin[r + qs:] = vn_ref[0]

    scale = A_HEAD_DIM ** -0.5
    lane = lax.broadcasted_iota(jnp.int32, (A_QSUB, 128), 1)

    def sub(s, carry):
        row0 = pl.multiple_of(s * A_QSUB, A_QSUB)
        rows = pl.ds(row0, A_QSUB)
        ku = i * qs + row0 - r + lax.broadcasted_iota(jnp.int32, (1, A_KWIN), 1)
        edge = jnp.where((ku >= 0) & (ku < seq_l), 0.0, NEG_BIG).astype(F32)
        lse_tile = jnp.zeros((A_QSUB, 128), F32)
        for h in range(A_HEADS):
            cs = slice(h * A_HEAD_DIM, (h + 1) * A_HEAD_DIM)
            q = q_ref[0, rows, cs]
            k = kwin[pl.ds(row0, A_KWIN), cs]
            v = vwin[pl.ds(row0, A_KWIN), cs]
            sc = _dot_nt(q, k) * scale + bias_ref[h] + edge
            m = jnp.max(sc, axis=-1, keepdims=True)
            p = jnp.exp(sc - m)
            l = jnp.sum(p, axis=-1, keepdims=True)
            o = jnp.dot(p.astype(BF16), v, preferred_element_type=F32) / l
            lse = m + jnp.log(l)
            if has_prev:
                lp = lprev_ref[0, rows, h:h + 1]
                op = oprev_ref[0, rows, cs]
                mx = jnp.maximum(lp, lse)
                a = jnp.exp(lp - mx)
                b = jnp.exp(lse - mx)
                den = a + b
                o = (op * a + o * b) / den
                lse = mx + jnp.log(den)
            if is_last:
                y_scr[rows, cs] = (o * _silu(gate_ref[0, rows, cs])).astype(BF16)
            else:
                o_ref[0, rows, cs] = o
                lse_tile = jnp.where(lane == h, lse, lse_tile)
        if not is_last:
            lse_ref[0, rows, :] = lse_tile
        return carry

    lax.fori_loop(0, qs // A_QSUB, sub, 0)

    if is_last:
        xnew_ref[0] = x_ref[0] + jnp.dot(y_scr[...], wout_ref[...],
                                         preferred_element_type=F32)


def _alibi_bias_table(group, dil):
    n = len(A_GROUPS) * A_HEADS
    slopes = (2.0 ** (-ALIBI_MAX_EXP * np.arange(1, n + 1) / n)).astype(np.float32)
    slopes = slopes.reshape(len(A_GROUPS), A_HEADS)[group]
    i = np.arange(A_QSUB)[:, None]
    j = np.arange(A_KWIN)[None, :]
    delta = np.abs(j - i - A_RADIUS)
    bias = (-slopes[:, None, None] * (delta * dil)[None]).astype(np.float32)
    return np.where((delta <= A_RADIUS)[None], bias, np.float32(NEG_BIG)).astype(np.float32)


def _win_attn(qkv, group, prev, last):
    b, s, _ = qkv.shape
    dil = A_GROUPS[group][1]
    seq_l = s // dil
    qs = min(512, seq_l)
    nq = seq_l // qs
    hb = qs // A_RADIUS
    nhb = seq_l // A_RADIUS
    gw = A_GROUP_WIDTH
    qkv_v = qkv.reshape(b, seq_l, dil * A_NQKV * gw)

    def col(n):
        return lambda bb, rr, ii: (bb, ii, rr * A_NQKV + 3 * group + n)

    def col_prev(n):
        return lambda bb, rr, ii: (bb, jnp.maximum(ii * hb - 1, 0), rr * A_NQKV + 3 * group + n)

    def col_next(n):
        return lambda bb, rr, ii: (bb, jnp.minimum((ii + 1) * hb, nhb - 1),
                                   rr * A_NQKV + 3 * group + n)

    def row_blk(bb, rr, ii):
        return (bb, ii, rr)

    in_specs = [pl.BlockSpec((1, qs, gw), col(0))]
    args = [qkv_v]
    for n in (1, 2):
        in_specs += [pl.BlockSpec((1, A_RADIUS, gw), col_prev(n)),
                     pl.BlockSpec((1, qs, gw), col(n)),
                     pl.BlockSpec((1, A_RADIUS, gw), col_next(n))]
        args += [qkv_v, qkv_v, qkv_v]
    in_specs.append(pl.BlockSpec((A_HEADS, A_QSUB, A_KWIN), lambda bb, rr, ii: (0, 0, 0)))
    args.append(jnp.asarray(_alibi_bias_table(group, dil)))
    if prev is not None:
        o_prev, l_prev = prev
        in_specs += [pl.BlockSpec((1, qs, gw), row_blk), pl.BlockSpec((1, qs, 128), row_blk)]
        args += [o_prev.reshape(b, seq_l, dil * gw), l_prev.reshape(b, seq_l, dil * 128)]
    scratch = [pltpu.VMEM((qs + 2 * A_RADIUS, gw), BF16),
               pltpu.VMEM((qs + 2 * A_RADIUS, gw), BF16)]
    if last is not None:
        gate, x, w_out = last
        in_specs += [pl.BlockSpec((1, qs, gw), row_blk),
                     pl.BlockSpec((1, qs, D_MODEL), row_blk),
                     pl.BlockSpec((gw, D_MODEL), lambda bb, rr, ii: (0, 0))]
        args += [gate.reshape(b, seq_l, dil * gw), x.reshape(b, seq_l, dil * D_MODEL), w_out]
        out_specs = pl.BlockSpec((1, qs, D_MODEL), row_blk)
        out_shape = jax.ShapeDtypeStruct((b, seq_l, dil * D_MODEL), F32)
        scratch.append(pltpu.VMEM((qs, gw), BF16))
    else:
        out_specs = [pl.BlockSpec((1, qs, gw), row_blk), pl.BlockSpec((1, qs, 128), row_blk)]
        out_shape = [jax.ShapeDtypeStruct((b, seq_l, dil * gw), F32),
                     jax.ShapeDtypeStruct((b, seq_l, dil * 128), F32)]

    out = pl.pallas_call(
        functools.partial(_win_attn_kernel, qs=qs, seq_l=seq_l,
                          has_prev=prev is not None, is_last=last is not None),
        grid=(b, dil, nq),
        in_specs=in_specs,
        out_specs=out_specs,
        out_shape=out_shape,
        scratch_shapes=scratch,
        compiler_params=_params("parallel", "parallel", "arbitrary"),
        name=f"a_attn_g{group}",
    )(*args)
    if last is not None:
        return out.reshape(b, s, D_MODEL)
    o, lse = out
    return o.reshape(b, s, gw), lse.reshape(b, s, 128)


def _dilated_layer(x, g, w_in_bf, w_out_bf):
    b, s, _ = x.shape
    qkv, gate = _a_proj(x.reshape(b * s, D_MODEL), g, w_in_bf)
    qkv = qkv.reshape(b, s, A_NQKV * A_GROUP_WIDTH)
    gate = gate.reshape(b, s, A_GROUP_WIDTH)
    st = _win_attn(qkv, 0, None, None)
    st = _win_attn(qkv, 1, st, None)
    return _win_attn(qkv, 2, st, (gate, x, w_out_bf))


def _rope(t, ca, cm, cp):
    return t * ca + pltpu.roll(t, 128 - B_ROPE // 2, 1) * cm + pltpu.roll(t, B_ROPE // 2, 1) * cp


def _b_proj_kernel(x_ref, g_ref, w1_ref, qg_ref, wq_ref, kvg_ref, wkv_ref,
                   ca_ref, cm_ref, cp_ref, q_ref, k_ref, v_ref, gate_ref):
    h = _rms(x_ref[0], g_ref[...]).astype(BF16)
    proj = jnp.dot(h, w1_ref[...], preferred_element_type=F32)
    c_q = proj[:, :B_Q_LORA]
    c_kv = proj[:, B_Q_LORA:B_Q_LORA + B_KV_LORA]
    k_r = proj[:, B_Q_LORA + B_KV_LORA:B_Q_LORA + B_KV_LORA + B_QK_PAD]
    gate_ref[0] = proj[:, B_Q_LORA + B_KV_LORA + B_QK_PAD:]
    ca, cm, cp = ca_ref[...], cm_ref[...], cp_ref[...]

    qn = _rms(c_q, qg_ref[...]).astype(BF16)
    q = jnp.dot(qn, wq_ref[...], preferred_element_type=F32)
    for hh in range(B_HEADS):
        cs = slice(hh * B_QK_PAD, (hh + 1) * B_QK_PAD)
        q_ref[0, :, cs] = _rope(q[:, cs], ca, cm, cp).astype(BF16)

    kvn = _rms(c_kv, kvg_ref[...]).astype(BF16)
    kv = jnp.dot(kvn, wkv_ref[...], preferred_element_type=F32)
    k_rot = _rope(k_r, ca, cm, cp)
    for hh in range(B_HEADS):
        cs = slice(hh * B_QK_PAD, (hh + 1) * B_QK_PAD)
        k_ref[0, :, cs] = (kv[:, cs] + k_rot).astype(BF16)
    v_ref[0] = kv[:, B_HEADS * B_QK_PAD:].astype(BF16)


def _rope_coeffs(s):
    half = B_ROPE // 2
    inv_freq = (ROPE_THETA ** (-np.arange(0, B_ROPE, 2) / B_ROPE)).astype(np.float32)
    ang = jnp.arange(s, dtype=F32)[:, None] * jnp.asarray(inv_freq)[None, :]
    cos, sin = jnp.cos(ang), jnp.sin(ang)
    ones = jnp.ones((s, B_NOPE), F32)
    zn = jnp.zeros((s, B_NOPE), F32)
    zh = jnp.zeros((s, half), F32)
    zt = jnp.zeros((s, B_QK_PAD - B_NOPE - B_ROPE), F32)
    ca = jnp.concatenate([ones, cos, cos, zt], axis=1)
    cm = jnp.concatenate([zn, -sin, zh, zt], axis=1)
    cp = jnp.concatenate([zn, zh, sin, zt], axis=1)
    return ca, cm, cp


def _b_proj(x, g, w1, qg, wq, kvg, wkv, coeffs, tm=512):
    b, s, _ = x.shape
    n1 = w1.shape[1]
    const = lambda bb, ii: (0, 0)
    tok = lambda bb, ii: (bb, ii, 0)
    qk_w = B_HEADS * B_QK_PAD
    v_w = B_HEADS * B_V
    return pl.pallas_call(
        _b_proj_kernel,
        grid=(b, s // tm),
        in_specs=[
            pl.BlockSpec((1, tm, D_MODEL), tok),
            pl.BlockSpec((1, D_MODEL), const),
            pl.BlockSpec((D_MODEL, n1), const),
            pl.BlockSpec((1, B_Q_LORA), const),
            pl.BlockSpec((B_Q_LORA, qk_w), const),
            pl.BlockSpec((1, B_KV_LORA), const),
            pl.BlockSpec((B_KV_LORA, qk_w + v_w), const),
            pl.BlockSpec((tm, B_QK_PAD), lambda bb, ii: (ii, 0)),
            pl.BlockSpec((tm, B_QK_PAD), lambda bb, ii: (ii, 0)),
            pl.BlockSpec((tm, B_QK_PAD), lambda bb, ii: (ii, 0)),
        ],
        out_specs=[
            pl.BlockSpec((1, tm, qk_w), tok),
            pl.BlockSpec((1, tm, qk_w), tok),
            pl.BlockSpec((1, tm, v_w), tok),
            pl.BlockSpec((1, tm, v_w), tok),
        ],
        out_shape=[
            jax.ShapeDtypeStruct((b, s, qk_w), BF16),
            jax.ShapeDtypeStruct((b, s, qk_w), BF16),
            jax.ShapeDtypeStruct((b, s, v_w), BF16),
            jax.ShapeDtypeStruct((b, s, v_w), F32),
        ],
        compiler_params=_params("parallel", "parallel"),
        name="b_proj",
    )(x, g, w1, qg, wq, kvg, wkv, *coeffs)


def _mla_attn_kernel(q_ref, k_ref, v_ref, gate_ref, y_ref, *, tk):
    qb = q_ref.shape[1]
    s = k_ref.shape[1]
    scale = (B_NOPE + B_ROPE) ** -0.5
    outs = []
    for hd in range(2):
        cs = slice(hd * B_QK_PAD, (hd + 1) * B_QK_PAD)
        qh = q_ref[0, :, cs]

        def chunk(c, carry):
            m, l, acc = carry
            rows = pl.ds(pl.multiple_of(c * tk, tk), tk)
            sc = _dot_nt(qh, k_ref[0, rows, cs]) * scale
            mn = jnp.maximum(m, jnp.max(sc, axis=-1, keepdims=True))
            alpha = jnp.exp(m - mn)
            p = jnp.exp(sc - mn)
            l = l * alpha + jnp.sum(p, axis=-1, keepdims=True)
            acc = acc * alpha + jnp.dot(p.astype(BF16), v_ref[0, rows, :],
                                        preferred_element_type=F32)
            return mn, l, acc

        init = (jnp.full((qb, 1), NEG_BIG, F32), jnp.zeros((qb, 1), F32),
                jnp.zeros((qb, 2 * B_V), F32))
        _, l, acc = lax.fori_loop(0, s // tk, chunk, init)
        outs.append(acc / l)
    lane = lax.broadcasted_iota(jnp.int32, (qb, 2 * B_V), 1)
    o = jnp.where(lane < B_V, outs[0], outs[1])
    y_ref[0] = (o * _silu(gate_ref[0])).astype(BF16)


def _mla_attn(q, k, v, gate, qb=512, tk=512):
    b, s, _ = q.shape
    blk = lambda bb, hp, ii: (bb, ii, hp)
    full = lambda bb, hp, ii: (bb, 0, hp)
    return pl.pallas_call(
        functools.partial(_mla_attn_kernel, tk=tk),
        grid=(b, B_HEADS // 2, s // qb),
        in_specs=[
            pl.BlockSpec((1, qb, 2 * B_QK_PAD), blk),
            pl.BlockSpec((1, s, 2 * B_QK_PAD), full),
            pl.BlockSpec((1, s, 2 * B_V), full),
            pl.BlockSpec((1, qb, 2 * B_V), blk),
        ],
        out_specs=pl.BlockSpec((1, qb, 2 * B_V), blk),
        out_shape=jax.ShapeDtypeStruct((b, s, B_HEADS * B_V), BF16),
        compiler_params=_params("parallel", "parallel", "arbitrary"),
        name="b_attn",
    )(q, k, v, gate)


def _out_proj_kernel(x_ref, y_ref, w_ref, fg_ref, o_ref, *, final_norm):
    xn = x_ref[...] + jnp.dot(y_ref[...], w_ref[...], preferred_element_type=F32)
    if final_norm:
        xn = _rms(xn, fg_ref[...])
    o_ref[...] = xn


def _out_proj(x2, y2, w_bf, fg, final_norm, tm=1024):
    t = x2.shape[0]
    return pl.pallas_call(
        functools.partial(_out_proj_kernel, final_norm=final_norm),
        grid=(t // tm,),
        in_specs=[
            pl.BlockSpec((tm, D_MODEL), lambda i: (i, 0)),
            pl.BlockSpec((tm, D_MODEL), lambda i: (i, 0)),
            pl.BlockSpec((D_MODEL, D_MODEL), lambda i: (0, 0)),
            pl.BlockSpec((1, D_MODEL), lambda i: (0, 0)),
        ],
        out_specs=pl.BlockSpec((tm, D_MODEL), lambda i: (i, 0)),
        out_shape=jax.ShapeDtypeStruct((t, D_MODEL), F32),
        compiler_params=_params("parallel"),
        name="b_out_proj",
    )(x2, y2, w_bf, fg)


def _mla_layer(x, g, wts, coeffs, fg, final_norm):
    b, s, _ = x.shape
    w1, qg, wq, kvg, wkv, w_out = wts
    q, k, v, gate = _b_proj(x, g, w1, qg, wq, kvg, wkv, coeffs)
    y = _mla_attn(q, k, v, gate)
    out = _out_proj(x.reshape(b * s, D_MODEL), y.reshape(b * s, D_MODEL), w_out, fg, final_norm)
    return out.reshape(b, s, D_MODEL)


def _prep_mla_weights(w_in, q_g, w_q_up, kv_g, w_kv_up, w_out):
    c0, c1, c2 = B_Q_LORA, B_Q_LORA + B_KV_LORA, B_Q_LORA + B_KV_LORA + B_ROPE
    pad_r = B_QK_PAD - B_NOPE - B_ROPE
    w_kr = jnp.pad(w_in[:, c1:c2], ((0, 0), (B_NOPE, pad_r)))
    w1 = jnp.concatenate([w_in[:, :c1], w_kr, w_in[:, c2:]], axis=1).astype(BF16)
    wq = w_q_up.reshape(B_Q_LORA, B_HEADS, B_NOPE + B_ROPE)
    wq = jnp.pad(wq, ((0, 0), (0, 0), (0, pad_r))).reshape(B_Q_LORA, B_HEADS * B_QK_PAD)
    wkv = w_kv_up.reshape(B_KV_LORA, B_HEADS, B_NOPE + B_V)
    wk = jnp.pad(wkv[:, :, :B_NOPE], ((0, 0), (0, 0), (0, B_QK_PAD - B_NOPE)))
    wk = wk.reshape(B_KV_LORA, B_HEADS * B_QK_PAD)
    wv = wkv[:, :, B_NOPE:].reshape(B_KV_LORA, B_HEADS * B_V)
    wkv2 = jnp.concatenate([wk, wv], axis=1)
    return (w1, q_g.reshape(1, -1), wq.astype(BF16), kv_g.reshape(1, -1),
            wkv2.astype(BF16), w_out.astype(BF16))


def _trunk(x, norm_g, a_wts, b_wts, coeffs, fg):
    for i in range(DEPTH):
        g = norm_g[i].reshape(1, D_MODEL)
        j = i // 2
        if i % 2 == 0:
            x = _dilated_layer(x, g, *a_wts[j])
        else:
            x = _mla_layer(x, g, b_wts[j], coeffs, fg, final_norm=(i == DEPTH - 1))
    return x


def kernel(x_prompt, x_sample, norm_g, a_w_in, a_w_out, b_w_in, b_q_norm_g, b_w_q_up,
           b_kv_norm_g, b_w_kv_up, b_w_out, final_norm_g):
    a_wts = [(a_w_in[j].astype(BF16), a_w_out[j].astype(BF16)) for j in range(a_w_in.shape[0])]
    b_wts = [_prep_mla_weights(b_w_in[j], b_q_norm_g[j], b_w_q_up[j], b_kv_norm_g[j],
                               b_w_kv_up[j], b_w_out[j]) for j in range(b_w_in.shape[0])]
    fg = final_norm_g.reshape(1, D_MODEL)
    outs = []
    for x in (x_prompt, x_sample):
        coeffs = _rope_coeffs(x.shape[1])
        outs.append(_trunk(x, norm_g, a_wts, b_wts, coeffs, fg))
    return tuple(outs)
```

```python
import functools

import numpy as np
import jax
import jax.numpy as jnp
from jax import lax
from jax.experimental import pallas as pl
from jax.experimental.pallas import tpu as pltpu

F32 = jnp.float32
BF16 = jnp.bfloat16

D_MODEL = 1024
DEPTH = 4
RMS_EPS = 1e-6
NEG_BIG = -1e30
LOG2E = float(np.log2(np.e))
LANES = 128

A_GROUPS = ((128, 1), (512, 4), (2048, 16))
A_DILS = tuple(d for _, d in A_GROUPS)
A_HEADS = 8
A_HEAD_DIM = 128
A_GROUP_WIDTH = A_HEADS * A_HEAD_DIM
A_NQKV = 3 * len(A_GROUPS)
ALIBI_MAX_EXP = 8.0
A_RADIUS = 64
A_QSUB = 128
A_KWIN = A_QSUB + 2 * A_RADIUS
A_TILES_PER_ITER = 4
A_Q_PRESCALE = float(LOG2E * A_HEAD_DIM ** -0.5)

B_HEADS = 16
B_NOPE = 64
B_ROPE = 32
B_V = 64
B_Q_LORA = 384
B_KV_LORA = 256
B_QK_PAD = 128
ROPE_THETA = 10000.0
B_Q_PRESCALE = float(LOG2E * (B_NOPE + B_ROPE) ** -0.5)

VMEM_LIMIT = 48 * 1024 * 1024


def _params(*sem):
    return pltpu.CompilerParams(dimension_semantics=sem, vmem_limit_bytes=VMEM_LIMIT)


def _rms(xf, g):
    ms = jnp.mean(xf * xf, axis=-1, keepdims=True)
    return xf * lax.rsqrt(ms + RMS_EPS) * g


def _silu(g):
    return g * (1.0 / (1.0 + jnp.exp(-g)))


def _dot_nt(a, b):
    return lax.dot_general(a, b, (((1,), (1,)), ((), ())), preferred_element_type=F32)


def _a_proj_kernel(x_ref, g_ref, w_ref, qkv0_ref, qkv1_ref, qkv2_ref, gate_ref,
                   h_scr, h1_scr, h2_scr, hf_scr):
    j = pl.program_id(2)
    tm = x_ref.shape[1]
    nslab = D_MODEL // LANES

    @pl.when(j == 0)
    def _():
        h = _rms(x_ref[0], g_ref[...])
        h_scr[...] = h.astype(BF16)
        for c in range(nslab):
            hf_scr[c * tm:(c + 1) * tm, :] = h[:, c * LANES:(c + 1) * LANES]
        for dil, dst in ((A_DILS[1], h1_scr), (A_DILS[2], h2_scr)):
            n = tm // dil
            for r in range(dil):
                for c in range(nslab):
                    dst[r * n:(r + 1) * n, c * LANES:(c + 1) * LANES] = (
                        hf_scr[pl.ds(c * tm + r, n, stride=dil), :].astype(BF16))

    def project(lhs_scr):
        acc = jnp.dot(lhs_scr[...], w_ref[...], preferred_element_type=F32)
        return acc * jnp.where(j % 3 == 0, A_Q_PRESCALE, 1.0)

    @pl.when(j < 3)
    def _():
        qkv0_ref[0, 0] = project(h_scr).astype(BF16)

    for lo, lhs_scr, out_ref, dil in ((3, h1_scr, qkv1_ref, A_DILS[1]),
                                      (6, h2_scr, qkv2_ref, A_DILS[2])):
        @pl.when((j >= lo) & (j < lo + 3))
        def _(lhs_scr=lhs_scr, out_ref=out_ref, dil=dil):
            acc = project(lhs_scr).astype(BF16)
            n = tm // dil
            for r in range(dil):
                out_ref[0, r] = acc[r * n:(r + 1) * n]

    @pl.when(j == A_NQKV)
    def _():
        gate_ref[0] = jnp.dot(h_scr[...], w_ref[...], preferred_element_type=F32)


def _a_proj(x, g, w_bf, tm=1024):
    b, s, _ = x.shape
    gw = A_GROUP_WIDTH

    def qkv_spec(group):
        dil = A_DILS[group]
        return pl.BlockSpec((1, dil, tm // dil, gw),
                            lambda bb, ii, jj: (bb, 0, ii, jnp.clip(jj - 3 * group, 0, 2)))

    return pl.pallas_call(
        _a_proj_kernel,
        grid=(b, s // tm, A_NQKV + 1),
        in_specs=[
            pl.BlockSpec((1, tm, D_MODEL), lambda bb, ii, jj: (bb, ii, 0)),
            pl.BlockSpec((1, D_MODEL), lambda bb, ii, jj: (0, 0)),
            pl.BlockSpec((D_MODEL, gw), lambda bb, ii, jj: (0, jj)),
        ],
        out_specs=[qkv_spec(0), qkv_spec(1), qkv_spec(2),
                   pl.BlockSpec((1, tm, gw), lambda bb, ii, jj: (bb, ii, 0))],
        out_shape=[jax.ShapeDtypeStruct((b, d, s // d, 3 * gw), BF16) for d in A_DILS]
        + [jax.ShapeDtypeStruct((b, s, gw), F32)],
        scratch_shapes=[pltpu.VMEM((tm, D_MODEL), BF16),
                        pltpu.VMEM((tm, D_MODEL), BF16),
                        pltpu.VMEM((tm, D_MODEL), BF16),
                        pltpu.VMEM((D_MODEL // LANES * tm, LANES), F32)],
        compiler_params=_params("parallel", "parallel", "arbitrary"),
        name="a_proj",
    )(x, g, w_bf)


def _a_attn_kernel(q0, k0, v0, q1, k1, v1, q2, k2, v2, bias_ref, out_ref, m_scr, l_scr):
    o_ref = out_ref.at[0]
    seq = o_ref.shape[0]
    ntile = seq // A_QSUB

    def tile(group, refs, dil, t):
        q_ref, k_ref, v_ref = refs
        seq_l = seq // dil
        nsub = seq_l // A_QSUB
        r = t // nsub
        sub = t % nsub
        row0 = pl.multiple_of(sub * A_QSUB, A_QSUB)
        start = pl.multiple_of(jnp.clip(row0 - A_RADIUS, 0, seq_l - A_KWIN), A_RADIUS)
        case = jnp.where(sub == 0, 0, jnp.where(sub == nsub - 1, 2, 1))
        q = q_ref[0, r, pl.ds(row0, A_QSUB), :]
        k = k_ref[0, r, pl.ds(start, A_KWIN), :]
        v = v_ref[0, r, pl.ds(start, A_KWIN), :]
        sc = _dot_nt(q, k) + bias_ref[0, 3 * group + case]
        m = jnp.max(sc, axis=-1, keepdims=True)
        p = jnp.exp2(sc - m)
        l = jnp.sum(p, axis=-1, keepdims=True)
        acc = jnp.dot(p.astype(BF16), v, preferred_element_type=F32)
        if dil == 1:
            rows = pl.ds(row0, A_QSUB)
        else:
            rows = pl.ds(row0 * dil + r, A_QSUB, stride=dil)
        return rows, m, l, acc

    for group, refs in enumerate(((q0, k0, v0), (q1, k1, v1), (q2, k2, v2))):
        dil = A_DILS[group]

        def step(it, carry, group=group, refs=refs, dil=dil):
            stats = [tile(group, refs, dil, it * A_TILES_PER_ITER + u)
                     for u in range(A_TILES_PER_ITER)]
            for rows, m, l, acc in stats:
                if group == 0:
                    m_scr[rows, :] = jnp.broadcast_to(m, (A_QSUB, LANES))
                    l_scr[rows, :] = jnp.broadcast_to(l, (A_QSUB, LANES))
                    o_ref[rows, :] = acc
                else:
                    m_old = m_scr[rows, :]
                    m_new = jnp.maximum(m_old, m)
                    a_old = jnp.exp2(m_old - m_new)
                    a_cur = jnp.exp2(m - m_new)
                    m_scr[rows, :] = m_new
                    l_scr[rows, :] = l_scr[rows, :] * a_old + l * a_cur
                    o_ref[rows, :] = o_ref[rows, :] * a_old + acc * a_cur
            return carry

        lax.fori_loop(0, ntile // A_TILES_PER_ITER, step, 0)

    o_ref[...] = o_ref[...] / l_scr[...]


def _alibi_bias_tables():
    n = len(A_GROUPS) * A_HEADS
    slopes = (2.0 ** (-ALIBI_MAX_EXP * np.arange(1, n + 1) / n)).astype(np.float32)
    slopes = slopes.reshape(len(A_GROUPS), A_HEADS)
    i = np.arange(A_QSUB)[:, None]
    j = np.arange(A_KWIN)[None, :]
    out = np.empty((A_HEADS, 3 * len(A_GROUPS), A_QSUB, A_KWIN), np.float32)
    for g, dil in enumerate(A_DILS):
        for case, shift in enumerate((0, -A_RADIUS, -2 * A_RADIUS)):
            delta = np.abs(j - i + shift)
            bias = (-slopes[g][:, None, None] * (delta * dil)[None]).astype(np.float32)
            bias = bias * np.float32(LOG2E)
            out[:, 3 * g + case] = np.where((delta <= A_RADIUS)[None], bias, np.float32(NEG_BIG))
    return out


def _a_attn(qkvs, seq):
    b = qkvs[0].shape[0]
    in_specs, args = [], []
    for qkv, dil in zip(qkvs, A_DILS):
        for n in range(3):
            in_specs.append(pl.BlockSpec((1, dil, seq // dil, A_HEAD_DIM),
                                         lambda bb, hh, n=n: (bb, 0, 0, n * A_HEADS + hh)))
            args.append(qkv)
    in_specs.append(pl.BlockSpec((1, 3 * len(A_GROUPS), A_QSUB, A_KWIN),
                                 lambda bb, hh: (hh, 0, 0, 0)))
    args.append(jnp.asarray(_alibi_bias_tables()))
    return pl.pallas_call(
        _a_attn_kernel,
        grid=(b, A_HEADS),
        in_specs=in_specs,
        out_specs=pl.BlockSpec((1, seq, A_HEAD_DIM), lambda bb, hh: (bb, 0, hh)),
        out_shape=jax.ShapeDtypeStruct((b, seq, A_GROUP_WIDTH), F32),
        scratch_shapes=[pltpu.VMEM((seq, LANES), F32), pltpu.VMEM((seq, LANES), F32)],
        compiler_params=_params("parallel", "arbitrary"),
        name="a_attn",
    )(*args)


def _dilated_layer(x, g, w_in_bf, w_out_bf, fg):
    b, s, _ = x.shape
    qkv0, qkv1, qkv2, gate = _a_proj(x, g, w_in_bf)
    o = _a_attn((qkv0, qkv1, qkv2), s)
    t = b * s
    out = _out_proj(x.reshape(t, D_MODEL), o.reshape(t, D_MODEL), gate.reshape(t, D_MODEL),
                    w_out_bf, fg, final_norm=False)
    return out.reshape(b, s, D_MODEL)


def _rope(t, ca, cm, cp):
    return t * ca + pltpu.roll(t, 128 - B_ROPE // 2, 1) * cm + pltpu.roll(t, B_ROPE // 2, 1) * cp


def _b_proj_kernel(x_ref, g_ref, w1_ref, qg_ref, wq_ref, kvg_ref, wk_ref, wvt_ref,
                   ca_ref, cm_ref, cp_ref, q_ref, k_ref, vt_ref, gate_ref):
    h = _rms(x_ref[0], g_ref[...]).astype(BF16)
    proj = jnp.dot(h, w1_ref[...], preferred_element_type=F32)
    c_q = proj[:, :B_Q_LORA]
    c_kv = proj[:, B_Q_LORA:B_Q_LORA + B_KV_LORA]
    k_r = proj[:, B_Q_LORA + B_KV_LORA:B_Q_LORA + B_KV_LORA + B_QK_PAD]
    gate_ref[0] = proj[:, B_Q_LORA + B_KV_LORA + B_QK_PAD:]
    ca, cm, cp = ca_ref[...], cm_ref[...], cp_ref[...]

    qn = _rms(c_q, qg_ref[...]).astype(BF16)
    q = jnp.dot(qn, wq_ref[...], preferred_element_type=F32)
    for hh in range(B_HEADS):
        cs = slice(hh * B_QK_PAD, (hh + 1) * B_QK_PAD)
        q_ref[0, :, cs] = (_rope(q[:, cs], ca, cm, cp) * B_Q_PRESCALE).astype(BF16)

    kvn = _rms(c_kv, kvg_ref[...]).astype(BF16)
    kn = jnp.dot(kvn, wk_ref[...], preferred_element_type=F32)
    k_rot = _rope(k_r, ca, cm, cp)
    for hh in range(B_HEADS):
        cs = slice(hh * B_QK_PAD, (hh + 1) * B_QK_PAD)
        k_ref[0, :, cs] = (kn[:, cs] + k_rot).astype(BF16)
    vt_ref[0] = _dot_nt(wvt_ref[...], kvn).astype(BF16)


def _rope_coeffs(s):
    half = B_ROPE // 2
    inv_freq = (ROPE_THETA ** (-np.arange(0, B_ROPE, 2) / B_ROPE)).astype(np.float32)
    ang = jnp.arange(s, dtype=F32)[:, None] * jnp.asarray(inv_freq)[None, :]
    cos, sin = jnp.cos(ang), jnp.sin(ang)
    ones = jnp.ones((s, B_NOPE), F32)
    zn = jnp.zeros((s, B_NOPE), F32)
    zh = jnp.zeros((s, half), F32)
    zt = jnp.zeros((s, B_QK_PAD - B_NOPE - B_ROPE), F32)
    ca = jnp.concatenate([ones, cos, cos, zt], axis=1)
    cm = jnp.concatenate([zn, -sin, zh, zt], axis=1)
    cp = jnp.concatenate([zn, zh, sin, zt], axis=1)
    return ca, cm, cp


def _b_proj(x, g, w1, qg, wq, kvg, wk, wvt, coeffs, tm=512):
    b, s, _ = x.shape
    n1 = w1.shape[1]
    const = lambda bb, ii: (0, 0)
    tok = lambda bb, ii: (bb, ii, 0)
    qk_w = B_HEADS * B_QK_PAD
    v_w = B_HEADS * B_V
    return pl.pallas_call(
        _b_proj_kernel,
        grid=(b, s // tm),
        in_specs=[
            pl.BlockSpec((1, tm, D_MODEL), tok),
            pl.BlockSpec((1, D_MODEL), const),
            pl.BlockSpec((D_MODEL, n1), const),
            pl.BlockSpec((1, B_Q_LORA), const),
            pl.BlockSpec((B_Q_LORA, qk_w), const),
            pl.BlockSpec((1, B_KV_LORA), const),
            pl.BlockSpec((B_KV_LORA, qk_w), const),
            pl.BlockSpec((v_w, B_KV_LORA), const),
            pl.BlockSpec((tm, B_QK_PAD), lambda bb, ii: (ii, 0)),
            pl.BlockSpec((tm, B_QK_PAD), lambda bb, ii: (ii, 0)),
            pl.BlockSpec((tm, B_QK_PAD), lambda bb, ii: (ii, 0)),
        ],
        out_specs=[
            pl.BlockSpec((1, tm, qk_w), tok),
            pl.BlockSpec((1, tm, qk_w), tok),
            pl.BlockSpec((1, v_w, tm), lambda bb, ii: (bb, 0, ii)),
            pl.BlockSpec((1, tm, v_w), tok),
        ],
        out_shape=[
            jax.ShapeDtypeStruct((b, s, qk_w), BF16),
            jax.ShapeDtypeStruct((b, s, qk_w), BF16),
            jax.ShapeDtypeStruct((b, v_w, s), BF16),
            jax.ShapeDtypeStruct((b, s, v_w), F32),
        ],
        compiler_params=_params("parallel", "parallel"),
        name="b_proj",
    )(x, g, w1, qg, wq, kvg, wk, wvt, *coeffs)


def _mla_attn_kernel(q_ref, k_ref, vt_ref, gate_ref, y_ref, st_scr, *, qb, tk):
    s = k_ref.shape[1]
    nq = s // qb
    nck = s // tk
    npair = nck // 2

    def keys_of(c):
        return pl.ds(pl.multiple_of(c * tk, tk), tk)

    def queries_of(qi):
        return pl.ds(pl.multiple_of(qi * qb, qb), qb)

    def scores(qi, c, slot):
        for hd in range(2):
            cs = slice(hd * B_QK_PAD, (hd + 1) * B_QK_PAD)
            st_scr[slot, hd] = _dot_nt(k_ref[0, keys_of(c), cs], q_ref[0, queries_of(qi), cs])

    def update(c, slot, carry):
        new = []
        for hd in range(2):
            m, l, acc = carry[hd]
            vrows = slice(hd * B_V, (hd + 1) * B_V)
            st = st_scr[slot, hd]
            mn = jnp.maximum(m, jnp.max(st, axis=0, keepdims=True))
            alpha = jnp.exp2(m - mn)
            p = jnp.exp2(st - mn)
            l = l * alpha + jnp.sum(p, axis=0, keepdims=True)
            acc = acc * alpha + jnp.dot(vt_ref[0, vrows, keys_of(c)], p.astype(BF16),
                                        preferred_element_type=F32)
            new.append((mn, l, acc))
        return tuple(new)

    def qblock(qi, _):
        def pair(i, carry):
            c0 = 2 * i
            scores(qi, c0 + 1, 1)
            carry = update(c0, 0, carry)
            wrap = c0 + 2 >= nck
            scores(jnp.where(wrap, jnp.minimum(qi + 1, nq - 1), qi),
                   jnp.where(wrap, 0, c0 + 2), 0)
            return update(c0 + 1, 1, carry)

        init = (jnp.full((1, qb), NEG_BIG, F32), jnp.zeros((1, qb), F32),
                jnp.zeros((B_V, qb), F32))
        res = lax.fori_loop(0, npair, pair, (init, init))
        outs = [acc / l for _, l, acc in res]
        o = jnp.concatenate(outs, axis=0).T
        rows = queries_of(qi)
        y_ref[0, rows, :] = (o * _silu(gate_ref[0, rows, :])).astype(BF16)
        return 0

    scores(0, 0, 0)
    lax.fori_loop(0, nq, qblock, 0)


def _mla_attn(q, k, vt, gate, qb=256, tk=512):
    b, s, _ = q.shape
    rows = lambda bb, hp: (bb, 0, hp)
    return pl.pallas_call(
        functools.partial(_mla_attn_kernel, qb=qb, tk=tk),
        grid=(b, B_HEADS // 2),
        in_specs=[
            pl.BlockSpec((1, s, 2 * B_QK_PAD), rows),
            pl.BlockSpec((1, s, 2 * B_QK_PAD), rows),
            pl.BlockSpec((1, 2 * B_V, s), lambda bb, hp: (bb, hp, 0)),
            pl.BlockSpec((1, s, 2 * B_V), rows),
        ],
        out_specs=pl.BlockSpec((1, s, 2 * B_V), rows),
        out_shape=jax.ShapeDtypeStruct((b, s, B_HEADS * B_V), BF16),
        scratch_shapes=[pltpu.VMEM((2, 2, tk, qb), F32)],
        compiler_params=_params("parallel", "arbitrary"),
        name="b_attn",
    )(q, k, vt, gate)


def _out_proj_kernel(*refs, gated, final_norm):
    if gated:
        x_ref, y_ref, gate_ref, w_ref, fg_ref, o_ref = refs
        y = (y_ref[...] * _silu(gate_ref[...])).astype(BF16)
    else:
        x_ref, y_ref, w_ref, fg_ref, o_ref = refs
        y = y_ref[...]
    xn = x_ref[...] + jnp.dot(y, w_ref[...], preferred_element_type=F32)
    if final_norm:
        xn = _rms(xn, fg_ref[...])
    o_ref[...] = xn


def _out_proj(x2, y2, gate2, w_bf, fg, final_norm, tm=1024):
    t = x2.shape[0]
    tok = pl.BlockSpec((tm, D_MODEL), lambda i: (i, 0))
    gated = gate2 is not None
    return pl.pallas_call(
        functools.partial(_out_proj_kernel, gated=gated, final_norm=final_norm),
        grid=(t // tm,),
        in_specs=[tok, tok] + ([tok] if gated else []) + [
            pl.BlockSpec((D_MODEL, D_MODEL), lambda i: (0, 0)),
            pl.BlockSpec((1, D_MODEL), lambda i: (0, 0)),
        ],
        out_specs=tok,
        out_shape=jax.ShapeDtypeStruct((t, D_MODEL), F32),
        compiler_params=_params("parallel"),
        name="out_proj",
    )(x2, y2, *([gate2] if gated else []), w_bf, fg)


def _mla_layer(x, g, wts, coeffs, fg, final_norm):
    b, s, _ = x.shape
    w1, qg, wq, kvg, wk, wvt, w_out = wts
    q, k, vt, gate = _b_proj(x, g, w1, qg, wq, kvg, wk, wvt, coeffs)
    y = _mla_attn(q, k, vt, gate)
    out = _out_proj(x.reshape(b * s, D_MODEL), y.reshape(b * s, D_MODEL), None, w_out, fg,
                    final_norm)
    return out.reshape(b, s, D_MODEL)


def _prep_mla_weights(w_in, q_g, w_q_up, kv_g, w_kv_up, w_out):
    c0, c1, c2 = B_Q_LORA, B_Q_LORA + B_KV_LORA, B_Q_LORA + B_KV_LORA + B_ROPE
    pad_r = B_QK_PAD - B_NOPE - B_ROPE
    w_kr = jnp.pad(w_in[:, c1:c2], ((0, 0), (B_NOPE, pad_r)))
    w1 = jnp.concatenate([w_in[:, :c1], w_kr, w_in[:, c2:]], axis=1).astype(BF16)
    wq = w_q_up.reshape(B_Q_LORA, B_HEADS, B_NOPE + B_ROPE)
    wq = jnp.pad(wq, ((0, 0), (0, 0), (0, pad_r))).reshape(B_Q_LORA, B_HEADS * B_QK_PAD)
    wkv = w_kv_up.reshape(B_KV_LORA, B_HEADS, B_NOPE + B_V)
    wk = jnp.pad(wkv[:, :, :B_NOPE], ((0, 0), (0, 0), (0, B_QK_PAD - B_NOPE)))
    wk = wk.reshape(B_KV_LORA, B_HEADS * B_QK_PAD)
    wvt = wkv[:, :, B_NOPE:].reshape(B_KV_LORA, B_HEADS * B_V).T
    return (w1, q_g.reshape(1, -1), wq.astype(BF16), kv_g.reshape(1, -1),
            wk.astype(BF16), wvt.astype(BF16), w_out.astype(BF16))


def _trunk(x, norm_g, a_wts, b_wts, coeffs, fg):
    for i in range(DEPTH):
        g = norm_g[i].reshape(1, D_MODEL)
        j = i // 2
        if i % 2 == 0:
            x = _dilated_layer(x, g, *a_wts[j], fg)
        else:
            x = _mla_layer(x, g, b_wts[j], coeffs, fg, final_norm=(i == DEPTH - 1))
    return x


def kernel(x_prompt, x_sample, norm_g, a_w_in, a_w_out, b_w_in, b_q_norm_g, b_w_q_up,
           b_kv_norm_g, b_w_kv_up, b_w_out, final_norm_g):
    a_wts = [(a_w_in[j].astype(BF16), a_w_out[j].astype(BF16)) for j in range(a_w_in.shape[0])]
    b_wts = [_prep_mla_weights(b_w_in[j], b_q_norm_g[j], b_w_q_up[j], b_kv_norm_g[j],
                               b_w_kv_up[j], b_w_out[j]) for j in range(b_w_in.shape[0])]
    fg = final_norm_g.reshape(1, D_MODEL)
    outs = []
    for x in (x_prompt, x_sample):
        coeffs = _rope_coeffs(x.shape[1])
        outs.append(_trunk(x, norm_g, a_wts, b_wts, coeffs, fg))
    return tuple(outs)
```

```python
import functools

import numpy as np
import jax
import jax.numpy as jnp
from jax import lax
from jax.experimental import pallas as pl
from jax.experimental.pallas import tpu as pltpu

F32 = jnp.float32
BF16 = jnp.bfloat16

D_MODEL = 1024
DEPTH = 4
RMS_EPS = 1e-6
NEG_BIG = -1e30
LOG2E = float(np.log2(np.e))
LANES = 128

A_GROUPS = ((128, 1), (512, 4), (2048, 16))
A_DILS = tuple(d for _, d in A_GROUPS)
A_HEADS = 8
A_HEAD_DIM = 128
A_GROUP_WIDTH = A_HEADS * A_HEAD_DIM
A_NQKV = 3 * len(A_GROUPS)
ALIBI_MAX_EXP = 8.0
A_RADIUS = 64
A_QSUB = 128
A_KWIN = A_QSUB + 2 * A_RADIUS
A_TILES_PER_ITER = 8
A_SCORE_AHEAD = 2
A_MERGE_BEHIND = 2
A_Q_PRESCALE = float(LOG2E * A_HEAD_DIM ** -0.5)

B_HEADS = 16
B_NOPE = 64
B_ROPE = 32
B_V = 64
B_Q_LORA = 384
B_KV_LORA = 256
B_QK_PAD = 128
B_ONES_ROWS = 16
ROPE_THETA = 10000.0
B_Q_PRESCALE = float(LOG2E * (B_NOPE + B_ROPE) ** -0.5)

VMEM_LIMIT = 48 * 1024 * 1024


def _params(*sem):
    return pltpu.CompilerParams(dimension_semantics=sem, vmem_limit_bytes=VMEM_LIMIT)


def _rms(xf, g):
    ms = jnp.mean(xf * xf, axis=-1, keepdims=True)
    return xf * lax.rsqrt(ms + RMS_EPS) * g


def _silu(g):
    return g * (1.0 / (1.0 + jnp.exp(-g)))


def _dot_nt(a, b):
    return lax.dot_general(a, b, (((1,), (1,)), ((), ())), preferred_element_type=F32)


def _a_proj_kernel(x_ref, g_ref, w_ref, qkv0_ref, qkv1_ref, qkv2_ref, gate_ref,
                   h_scr, h1_scr, h2_scr, hf_scr):
    j = pl.program_id(2)
    tm = x_ref.shape[1]
    nslab = D_MODEL // LANES

    @pl.when(j == 0)
    def _():
        h = _rms(x_ref[0], g_ref[...])
        h_scr[...] = h.astype(BF16)
        for c in range(nslab):
            hf_scr[c * tm:(c + 1) * tm, :] = h[:, c * LANES:(c + 1) * LANES]
        for dil, dst in ((A_DILS[1], h1_scr), (A_DILS[2], h2_scr)):
            n = tm // dil
            for r in range(dil):
                for c in range(nslab):
                    dst[r * n:(r + 1) * n, c * LANES:(c + 1) * LANES] = (
                        hf_scr[pl.ds(c * tm + r, n, stride=dil), :].astype(BF16))

    def project(lhs_scr):
        acc = jnp.dot(lhs_scr[...], w_ref[...], preferred_element_type=F32)
        return acc * jnp.where(j % 3 == 0, A_Q_PRESCALE, 1.0)

    @pl.when(j < 3)
    def _():
        qkv0_ref[0, 0] = project(h_scr).astype(BF16)

    for lo, lhs_scr, out_ref, dil in ((3, h1_scr, qkv1_ref, A_DILS[1]),
                                      (6, h2_scr, qkv2_ref, A_DILS[2])):
        @pl.when((j >= lo) & (j < lo + 3))
        def _(lhs_scr=lhs_scr, out_ref=out_ref, dil=dil):
            acc = project(lhs_scr).astype(BF16)
            n = tm // dil
            for r in range(dil):
                out_ref[0, r] = acc[r * n:(r + 1) * n]

    @pl.when(j == A_NQKV)
    def _():
        gate_ref[0] = jnp.dot(h_scr[...], w_ref[...], preferred_element_type=F32)


def _a_proj(x, g, w_bf, tm=1024):
    b, s, _ = x.shape
    gw = A_GROUP_WIDTH

    def qkv_spec(group):
        dil = A_DILS[group]
        return pl.BlockSpec((1, dil, tm // dil, gw),
                            lambda bb, ii, jj: (bb, 0, ii, jnp.clip(jj - 3 * group, 0, 2)))

    return pl.pallas_call(
        _a_proj_kernel,
        grid=(b, s // tm, A_NQKV + 1),
        in_specs=[
            pl.BlockSpec((1, tm, D_MODEL), lambda bb, ii, jj: (bb, ii, 0)),
            pl.BlockSpec((1, D_MODEL), lambda bb, ii, jj: (0, 0)),
            pl.BlockSpec((D_MODEL, gw), lambda bb, ii, jj: (0, jj)),
        ],
        out_specs=[qkv_spec(0), qkv_spec(1), qkv_spec(2),
                   pl.BlockSpec((1, tm, gw), lambda bb, ii, jj: (bb, ii, 0))],
        out_shape=[jax.ShapeDtypeStruct((b, d, s // d, 3 * gw), BF16) for d in A_DILS]
        + [jax.ShapeDtypeStruct((b, s, gw), F32)],
        scratch_shapes=[pltpu.VMEM((tm, D_MODEL), BF16),
                        pltpu.VMEM((tm, D_MODEL), BF16),
                        pltpu.VMEM((tm, D_MODEL), BF16),
                        pltpu.VMEM((D_MODEL // LANES * tm, LANES), F32)],
        compiler_params=_params("parallel", "parallel", "arbitrary"),
        name="a_proj",
    )(x, g, w_bf)


def _a_attn_kernel(q0, k0, v0, q1, k1, v1, q2, k2, v2, bias_ref, out_ref, m_scr, l_scr):
    o_ref = out_ref.at[0]
    seq = o_ref.shape[0]
    ntile = seq // A_QSUB

    def place(dil, t):
        seq_l = seq // dil
        nsub = seq_l // A_QSUB
        r = t // nsub
        sub = t % nsub
        row0 = pl.multiple_of(sub * A_QSUB, A_QSUB)
        start = pl.multiple_of(jnp.clip(row0 - A_RADIUS, 0, seq_l - A_KWIN), A_RADIUS)
        case = jnp.where(sub == 0, 0, jnp.where(sub == nsub - 1, 2, 1))
        if dil == 1:
            rows = pl.ds(row0, A_QSUB)
        else:
            rows = pl.ds(row0 * dil + r, A_QSUB, stride=dil)
        return r, row0, start, case, rows

    first = len(A_GROUPS) - 1
    for group, refs in reversed(tuple(enumerate(((q0, k0, v0), (q1, k1, v1), (q2, k2, v2))))):
        dil = A_DILS[group]

        def step(it, carry, group=group, refs=refs, dil=dil):
            q_ref, k_ref, v_ref = refs
            n = A_TILES_PER_ITER
            places = [place(dil, it * n + u) for u in range(n)]
            scs, stats = {}, {}

            def score(u):
                r, row0, start, case, _ = places[u]
                scs[u] = (_dot_nt(q_ref[0, r, pl.ds(row0, A_QSUB), :],
                                  k_ref[0, r, pl.ds(start, A_KWIN), :])
                          + bias_ref[0, 3 * group + case])

            def softmax_values(u):
                r, _, start, _, _ = places[u]
                sc = scs.pop(u)
                m = jnp.max(sc, axis=-1, keepdims=True)
                p = jnp.exp2(sc - m)
                l = jnp.sum(p, axis=-1, keepdims=True)
                acc = jnp.dot(p.astype(BF16), v_ref[0, r, pl.ds(start, A_KWIN), :],
                              preferred_element_type=F32)
                stats[u] = (m, l, acc)

            def merge(u):
                rows = places[u][4]
                m, l, acc = stats.pop(u)
                if group == first:
                    m_scr[rows, :] = jnp.broadcast_to(m, (A_QSUB, LANES))
                    l_scr[rows, :] = jnp.broadcast_to(l, (A_QSUB, LANES))
                    o_ref[rows, :] = acc
                else:
                    m_old = m_scr[rows, :]
                    m_new = jnp.maximum(m_old, m)
                    a_old = jnp.exp2(m_old - m_new)
                    a_cur = jnp.exp2(m - m_new)
                    m_scr[rows, :] = m_new
                    l_scr[rows, :] = l_scr[rows, :] * a_old + l * a_cur
                    o_ref[rows, :] = o_ref[rows, :] * a_old + acc * a_cur

            for s in range(n + A_SCORE_AHEAD + A_MERGE_BEHIND):
                if s < n:
                    score(s)
                if 0 <= s - A_SCORE_AHEAD < n:
                    softmax_values(s - A_SCORE_AHEAD)
                if 0 <= s - A_SCORE_AHEAD - A_MERGE_BEHIND < n:
                    merge(s - A_SCORE_AHEAD - A_MERGE_BEHIND)
            return carry

        lax.fori_loop(0, ntile // A_TILES_PER_ITER, step, 0)

    o_ref[...] = o_ref[...] / l_scr[...]


def _alibi_bias_tables():
    n = len(A_GROUPS) * A_HEADS
    slopes = (2.0 ** (-ALIBI_MAX_EXP * np.arange(1, n + 1) / n)).astype(np.float32)
    slopes = slopes.reshape(len(A_GROUPS), A_HEADS)
    i = np.arange(A_QSUB)[:, None]
    j = np.arange(A_KWIN)[None, :]
    out = np.empty((A_HEADS, 3 * len(A_GROUPS), A_QSUB, A_KWIN), np.float32)
    for g, dil in enumerate(A_DILS):
        for case, shift in enumerate((0, -A_RADIUS, -2 * A_RADIUS)):
            delta = np.abs(j - i + shift)
            bias = (-slopes[g][:, None, None] * (delta * dil)[None]).astype(np.float32)
            bias = bias * np.float32(LOG2E)
            out[:, 3 * g + case] = np.where((delta <= A_RADIUS)[None], bias, np.float32(NEG_BIG))
    return out


def _a_attn(qkvs, seq):
    b = qkvs[0].shape[0]
    in_specs, args = [], []
    for qkv, dil in zip(qkvs, A_DILS):
        for n in range(3):
            in_specs.append(pl.BlockSpec((1, dil, seq // dil, A_HEAD_DIM),
                                         lambda bb, hh, n=n: (bb, 0, 0, n * A_HEADS + hh)))
            args.append(qkv)
    in_specs.append(pl.BlockSpec((1, 3 * len(A_GROUPS), A_QSUB, A_KWIN),
                                 lambda bb, hh: (hh, 0, 0, 0)))
    args.append(jnp.asarray(_alibi_bias_tables()))
    return pl.pallas_call(
        _a_attn_kernel,
        grid=(b, A_HEADS),
        in_specs=in_specs,
        out_specs=pl.BlockSpec((1, seq, A_HEAD_DIM), lambda bb, hh: (bb, 0, hh)),
        out_shape=jax.ShapeDtypeStruct((b, seq, A_GROUP_WIDTH), F32),
        scratch_shapes=[pltpu.VMEM((seq, LANES), F32), pltpu.VMEM((seq, LANES), F32)],
        compiler_params=_params("parallel", "arbitrary"),
        name="a_attn",
    )(*args)


def _dilated_layer(x, g, w_in_bf, w_out_bf, fg):
    b, s, _ = x.shape
    qkv0, qkv1, qkv2, gate = _a_proj(x, g, w_in_bf)
    o = _a_attn((qkv0, qkv1, qkv2), s)
    t = b * s
    out = _out_proj(x.reshape(t, D_MODEL), o.reshape(t, D_MODEL), gate.reshape(t, D_MODEL),
                    w_out_bf, fg, final_norm=False)
    return out.reshape(b, s, D_MODEL)


def _rope(t, ca, cm, cp):
    return t * ca + pltpu.roll(t, 128 - B_ROPE // 2, 1) * cm + pltpu.roll(t, B_ROPE // 2, 1) * cp


def _b_proj_kernel(x_ref, g_ref, w1_ref, qg_ref, wq_ref, kvg_ref, wk_ref, wvt_ref,
                   ca_ref, cm_ref, cp_ref, q_ref, k_ref, vt_ref, gate_ref):
    h = _rms(x_ref[0], g_ref[...]).astype(BF16)
    proj = jnp.dot(h, w1_ref[...], preferred_element_type=F32)
    c_q = proj[:, :B_Q_LORA]
    c_kv = proj[:, B_Q_LORA:B_Q_LORA + B_KV_LORA]
    k_r = proj[:, B_Q_LORA + B_KV_LORA:B_Q_LORA + B_KV_LORA + B_QK_PAD]
    gate_ref[0] = proj[:, B_Q_LORA + B_KV_LORA + B_QK_PAD:]
    ca, cm, cp = ca_ref[...], cm_ref[...], cp_ref[...]

    qn = _rms(c_q, qg_ref[...]).astype(BF16)
    q = jnp.dot(qn, wq_ref[...], preferred_element_type=F32)
    for hh in range(B_HEADS):
        cs = slice(hh * B_QK_PAD, (hh + 1) * B_QK_PAD)
        q_ref[0, :, cs] = (_rope(q[:, cs], ca, cm, cp) * B_Q_PRESCALE).astype(BF16)

    kvn = _rms(c_kv, kvg_ref[...]).astype(BF16)
    kn = jnp.dot(kvn, wk_ref[...], preferred_element_type=F32)
    k_rot = _rope(k_r, ca, cm, cp)
    for hh in range(B_HEADS):
        cs = slice(hh * B_QK_PAD, (hh + 1) * B_QK_PAD)
        k_ref[0, :, cs] = (kn[:, cs] + k_rot).astype(BF16)
    vt_ref[0] = _dot_nt(wvt_ref[...], kvn).astype(BF16)


def _rope_coeffs(s):
    half = B_ROPE // 2
    inv_freq = (ROPE_THETA ** (-np.arange(0, B_ROPE, 2) / B_ROPE)).astype(np.float32)
    ang = jnp.arange(s, dtype=F32)[:, None] * jnp.asarray(inv_freq)[None, :]
    cos, sin = jnp.cos(ang), jnp.sin(ang)
    ones = jnp.ones((s, B_NOPE), F32)
    zn = jnp.zeros((s, B_NOPE), F32)
    zh = jnp.zeros((s, half), F32)
    zt = jnp.zeros((s, B_QK_PAD - B_NOPE - B_ROPE), F32)
    ca = jnp.concatenate([ones, cos, cos, zt], axis=1)
    cm = jnp.concatenate([zn, -sin, zh, zt], axis=1)
    cp = jnp.concatenate([zn, zh, sin, zt], axis=1)
    return ca, cm, cp


def _b_proj(x, g, w1, qg, wq, kvg, wk, wvt, coeffs, tm=512):
    b, s, _ = x.shape
    n1 = w1.shape[1]
    const = lambda bb, ii: (0, 0)
    tok = lambda bb, ii: (bb, ii, 0)
    qk_w = B_HEADS * B_QK_PAD
    v_w = B_HEADS * B_V
    return pl.pallas_call(
        _b_proj_kernel,
        grid=(b, s // tm),
        in_specs=[
            pl.BlockSpec((1, tm, D_MODEL), tok),
            pl.BlockSpec((1, D_MODEL), const),
            pl.BlockSpec((D_MODEL, n1), const),
            pl.BlockSpec((1, B_Q_LORA), const),
            pl.BlockSpec((B_Q_LORA, qk_w), const),
            pl.BlockSpec((1, B_KV_LORA), const),
            pl.BlockSpec((B_KV_LORA, qk_w), const),
            pl.BlockSpec((v_w, B_KV_LORA), const),
            pl.BlockSpec((tm, B_QK_PAD), lambda bb, ii: (ii, 0)),
            pl.BlockSpec((tm, B_QK_PAD), lambda bb, ii: (ii, 0)),
            pl.BlockSpec((tm, B_QK_PAD), lambda bb, ii: (ii, 0)),
        ],
        out_specs=[
            pl.BlockSpec((1, tm, qk_w), tok),
            pl.BlockSpec((1, tm, qk_w), tok),
            pl.BlockSpec((1, v_w, tm), lambda bb, ii: (bb, 0, ii)),
            pl.BlockSpec((1, tm, v_w), tok),
        ],
        out_shape=[
            jax.ShapeDtypeStruct((b, s, qk_w), BF16),
            jax.ShapeDtypeStruct((b, s, qk_w), BF16),
            jax.ShapeDtypeStruct((b, v_w, s), BF16),
            jax.ShapeDtypeStruct((b, s, v_w), F32),
        ],
        compiler_params=_params("parallel", "parallel"),
        name="b_proj",
    )(x, g, w1, qg, wq, kvg, wk, wvt, *coeffs)


def _mla_attn_kernel(q_ref, k_ref, vt_ref, gate_ref, y_ref, st_scr, *, qb, tk):
    s = k_ref.shape[1]
    nq = s // qb
    nck = s // tk
    npair = nck // 2

    def keys_of(c):
        return pl.ds(pl.multiple_of(c * tk, tk), tk)

    def queries_of(qi):
        return pl.ds(pl.multiple_of(qi * qb, qb), qb)

    def scores(qi, c, slot):
        for hd in range(2):
            cs = slice(hd * B_QK_PAD, (hd + 1) * B_QK_PAD)
            st_scr[slot, hd] = _dot_nt(k_ref[0, keys_of(c), cs], q_ref[0, queries_of(qi), cs])

    ones_rows = jnp.ones((B_ONES_ROWS, tk), BF16)

    def update(c, slot, carry):
        new = []
        for hd in range(2):
            m, acc = carry[hd]
            vrows = slice(hd * B_V, (hd + 1) * B_V)
            st = st_scr[slot, hd]
            mn = jnp.maximum(m, jnp.max(st, axis=0, keepdims=True))
            alpha = jnp.exp2(m - mn)
            p = jnp.exp2(st - mn).astype(BF16)
            v1 = jnp.concatenate([vt_ref[0, vrows, keys_of(c)], ones_rows], axis=0)
            acc = acc * alpha + jnp.dot(v1, p, preferred_element_type=F32)
            new.append((mn, acc))
        return tuple(new)

    def qblock(qi, _):
        def pair(i, carry):
            c0 = 2 * i
            scores(qi, c0 + 1, 1)
            carry = update(c0, 0, carry)
            wrap = c0 + 2 >= nck
            scores(jnp.where(wrap, jnp.minimum(qi + 1, nq - 1), qi),
                   jnp.where(wrap, 0, c0 + 2), 0)
            return update(c0 + 1, 1, carry)

        init = (jnp.full((1, qb), NEG_BIG, F32), jnp.zeros((B_V + B_ONES_ROWS, qb), F32))
        res = lax.fori_loop(0, npair, pair, (init, init), unroll=True)
        outs = [acc[:B_V] / acc[B_V:B_V + 1] for _, acc in res]
        o = jnp.concatenate(outs, axis=0).T
        rows = queries_of(qi)
        y_ref[0, rows, :] = (o * _silu(gate_ref[0, rows, :])).astype(BF16)
        return 0

    scores(0, 0, 0)
    lax.fori_loop(0, nq, qblock, 0)


def _mla_attn(q, k, vt, gate, qb=256, tk=512):
    b, s, _ = q.shape
    rows = lambda bb, hp: (bb, 0, hp)
    return pl.pallas_call(
        functools.partial(_mla_attn_kernel, qb=qb, tk=tk),
        grid=(b, B_HEADS // 2),
        in_specs=[
            pl.BlockSpec((1, s, 2 * B_QK_PAD), rows),
            pl.BlockSpec((1, s, 2 * B_QK_PAD), rows),
            pl.BlockSpec((1, 2 * B_V, s), lambda bb, hp: (bb, hp, 0)),
            pl.BlockSpec((1, s, 2 * B_V), rows),
        ],
        out_specs=pl.BlockSpec((1, s, 2 * B_V), rows),
        out_shape=jax.ShapeDtypeStruct((b, s, B_HEADS * B_V), BF16),
        scratch_shapes=[pltpu.VMEM((2, 2, tk, qb), F32)],
        compiler_params=_params("parallel", "arbitrary"),
        name="b_attn",
    )(q, k, vt, gate)


def _out_proj_kernel(*refs, gated, final_norm):
    if gated:
        x_ref, y_ref, gate_ref, w_ref, fg_ref, o_ref = refs
        y = (y_ref[...] * _silu(gate_ref[...])).astype(BF16)
    else:
        x_ref, y_ref, w_ref, fg_ref, o_ref = refs
        y = y_ref[...]
    xn = x_ref[...] + jnp.dot(y, w_ref[...], preferred_element_type=F32)
    if final_norm:
        xn = _rms(xn, fg_ref[...])
    o_ref[...] = xn


def _out_proj(x2, y2, gate2, w_bf, fg, final_norm, tm=1024):
    t = x2.shape[0]
    tok = pl.BlockSpec((tm, D_MODEL), lambda i: (i, 0))
    gated = gate2 is not None
    return pl.pallas_call(
        functools.partial(_out_proj_kernel, gated=gated, final_norm=final_norm),
        grid=(t // tm,),
        in_specs=[tok, tok] + ([tok] if gated else []) + [
            pl.BlockSpec((D_MODEL, D_MODEL), lambda i: (0, 0)),
            pl.BlockSpec((1, D_MODEL), lambda i: (0, 0)),
        ],
        out_specs=tok,
        out_shape=jax.ShapeDtypeStruct((t, D_MODEL), F32),
        compiler_params=_params("parallel"),
        name="out_proj",
    )(x2, y2, *([gate2] if gated else []), w_bf, fg)


def _mla_layer(x, g, wts, coeffs, fg, final_norm):
    b, s, _ = x.shape
    w1, qg, wq, kvg, wk, wvt, w_out = wts
    q, k, vt, gate = _b_proj(x, g, w1, qg, wq, kvg, wk, wvt, coeffs)
    y = _mla_attn(q, k, vt, gate)
    out = _out_proj(x.reshape(b * s, D_MODEL), y.reshape(b * s, D_MODEL), None, w_out, fg,
                    final_norm)
    return out.reshape(b, s, D_MODEL)


def _prep_mla_weights(w_in, q_g, w_q_up, kv_g, w_kv_up, w_out):
    c0, c1, c2 = B_Q_LORA, B_Q_LORA + B_KV_LORA, B_Q_LORA + B_KV_LORA + B_ROPE
    pad_r = B_QK_PAD - B_NOPE - B_ROPE
    w_kr = jnp.pad(w_in[:, c1:c2], ((0, 0), (B_NOPE, pad_r)))
    w1 = jnp.concatenate([w_in[:, :c1], w_kr, w_in[:, c2:]], axis=1).astype(BF16)
    wq = w_q_up.reshape(B_Q_LORA, B_HEADS, B_NOPE + B_ROPE)
    wq = jnp.pad(wq, ((0, 0), (0, 0), (0, pad_r))).reshape(B_Q_LORA, B_HEADS * B_QK_PAD)
    wkv = w_kv_up.reshape(B_KV_LORA, B_HEADS, B_NOPE + B_V)
    wk = jnp.pad(wkv[:, :, :B_NOPE], ((0, 0), (0, 0), (0, B_QK_PAD - B_NOPE)))
    wk = wk.reshape(B_KV_LORA, B_HEADS * B_QK_PAD)
    wvt = wkv[:, :, B_NOPE:].reshape(B_KV_LORA, B_HEADS * B_V).T
    return (w1, q_g.reshape(1, -1), wq.astype(BF16), kv_g.reshape(1, -1),
            wk.astype(BF16), wvt.astype(BF16), w_out.astype(BF16))


def _trunk(x, norm_g, a_wts, b_wts, coeffs, fg):
    for i in range(DEPTH):
        g = norm_g[i].reshape(1, D_MODEL)
        j = i // 2
        if i % 2 == 0:
            x = _dilated_layer(x, g, *a_wts[j], fg)
        else:
            x = _mla_layer(x, g, b_wts[j], coeffs, fg, final_norm=(i == DEPTH - 1))
    return x


def kernel(x_prompt, x_sample, norm_g, a_w_in, a_w_out, b_w_in, b_q_norm_g, b_w_q_up,
           b_kv_norm_g, b_w_kv_up, b_w_out, final_norm_g):
    a_wts = [(a_w_in[j].astype(BF16), a_w_out[j].astype(BF16)) for j in range(a_w_in.shape[0])]
    b_wts = [_prep_mla_weights(b_w_in[j], b_q_norm_g[j], b_w_q_up[j], b_kv_norm_g[j],
                               b_w_kv_up[j], b_w_out[j]) for j in range(b_w_in.shape[0])]
    fg = final_norm_g.reshape(1, D_MODEL)
    outs = []
    for x in (x_prompt, x_sample):
        coeffs = _rope_coeffs(x.shape[1])
        outs.append(_trunk(x, norm_g, a_wts, b_wts, coeffs, fg))
    return tuple(outs)
```

```python
import functools

import numpy as np
import jax
import jax.numpy as jnp
from jax import lax
from jax.experimental import pallas as pl
from jax.experimental.pallas import tpu as pltpu

F32 = jnp.float32
BF16 = jnp.bfloat16

D_MODEL = 1024
DEPTH = 4
RMS_EPS = 1e-6
NEG_BIG = -1e30
LOG2E = float(np.log2(np.e))
LANES = 128

A_GROUPS = ((128, 1), (512, 4), (2048, 16))
A_DILS = tuple(d for _, d in A_GROUPS)
A_HEADS = 8
A_HEAD_DIM = 128
A_GROUP_WIDTH = A_HEADS * A_HEAD_DIM
A_NQKV = 3 * len(A_GROUPS)
ALIBI_MAX_EXP = 8.0
A_RADIUS = 64
A_QSUB = 128
A_KWIN = A_QSUB + 2 * A_RADIUS
A_TILES_PER_ITER = 8
A_SCORE_AHEAD = 2
A_MERGE_BEHIND = 2
A_Q_PRESCALE = float(LOG2E * A_HEAD_DIM ** -0.5)

B_HEADS = 16
B_NOPE = 64
B_ROPE = 32
B_V = 64
B_Q_LORA = 384
B_KV_LORA = 256
B_QK_PAD = 128
B_ONES_ROWS = 16
ROPE_THETA = 10000.0
B_Q_PRESCALE = float(LOG2E * (B_NOPE + B_ROPE) ** -0.5)

VMEM_LIMIT = 48 * 1024 * 1024


def _params(*sem):
    return pltpu.CompilerParams(dimension_semantics=sem, vmem_limit_bytes=VMEM_LIMIT)


def _rms(xf, g):
    ms = jnp.mean(xf * xf, axis=-1, keepdims=True)
    return xf * lax.rsqrt(ms + RMS_EPS) * g


def _silu(g):
    return g * (1.0 / (1.0 + jnp.exp(-g)))


def _dot_nt(a, b):
    return lax.dot_general(a, b, (((1,), (1,)), ((), ())), preferred_element_type=F32)


def _a_proj_kernel(x_ref, g_ref, w_ref, qkv0_ref, qkv1_ref, qkv2_ref, gate_ref,
                   h_scr, h1_scr, h2_scr, hf_scr):
    j = pl.program_id(2)
    tm = x_ref.shape[1]
    nslab = D_MODEL // LANES

    @pl.when(j == 0)
    def _():
        h = _rms(x_ref[0], g_ref[...])
        h_scr[...] = h.astype(BF16)
        for c in range(nslab):
            hf_scr[c * tm:(c + 1) * tm, :] = h[:, c * LANES:(c + 1) * LANES]
        for dil, dst in ((A_DILS[1], h1_scr), (A_DILS[2], h2_scr)):
            n = tm // dil
            for r in range(dil):
                for c in range(nslab):
                    dst[r * n:(r + 1) * n, c * LANES:(c + 1) * LANES] = (
                        hf_scr[pl.ds(c * tm + r, n, stride=dil), :].astype(BF16))

    def project(lhs_scr):
        acc = jnp.dot(lhs_scr[...], w_ref[...], preferred_element_type=F32)
        return acc * jnp.where(j % 3 == 0, A_Q_PRESCALE, 1.0)

    @pl.when(j < 3)
    def _():
        qkv0_ref[0, 0] = project(h_scr).astype(BF16)

    for lo, lhs_scr, out_ref, dil in ((3, h1_scr, qkv1_ref, A_DILS[1]),
                                      (6, h2_scr, qkv2_ref, A_DILS[2])):
        @pl.when((j >= lo) & (j < lo + 3))
        def _(lhs_scr=lhs_scr, out_ref=out_ref, dil=dil):
            acc = project(lhs_scr).astype(BF16)
            n = tm // dil
            for r in range(dil):
                out_ref[0, r] = acc[r * n:(r + 1) * n]

    @pl.when(j == A_NQKV)
    def _():
        gate_ref[0] = jnp.dot(h_scr[...], w_ref[...], preferred_element_type=F32)


def _a_proj(x, g, w_bf, tm=1024):
    b, s, _ = x.shape
    gw = A_GROUP_WIDTH

    def qkv_spec(group):
        dil = A_DILS[group]
        return pl.BlockSpec((1, dil, tm // dil, gw),
                            lambda bb, ii, jj: (bb, 0, ii, jnp.clip(jj - 3 * group, 0, 2)))

    return pl.pallas_call(
        _a_proj_kernel,
        grid=(b, s // tm, A_NQKV + 1),
        in_specs=[
            pl.BlockSpec((1, tm, D_MODEL), lambda bb, ii, jj: (bb, ii, 0)),
            pl.BlockSpec((1, D_MODEL), lambda bb, ii, jj: (0, 0)),
            pl.BlockSpec((D_MODEL, gw), lambda bb, ii, jj: (0, jj)),
        ],
        out_specs=[qkv_spec(0), qkv_spec(1), qkv_spec(2),
                   pl.BlockSpec((1, tm, gw), lambda bb, ii, jj: (bb, ii, 0))],
        out_shape=[jax.ShapeDtypeStruct((b, d, s // d, 3 * gw), BF16) for d in A_DILS]
        + [jax.ShapeDtypeStruct((b, s, gw), F32)],
        scratch_shapes=[pltpu.VMEM((tm, D_MODEL), BF16),
                        pltpu.VMEM((tm, D_MODEL), BF16),
                        pltpu.VMEM((tm, D_MODEL), BF16),
                        pltpu.VMEM((D_MODEL // LANES * tm, LANES), F32)],
        compiler_params=_params("parallel", "parallel", "arbitrary"),
        name="a_proj",
    )(x, g, w_bf)


def _a_attn_kernel(q0, k0, v0, q1, k1, v1, q2, k2, v2, bias_ref, out_ref, m_scr, l_scr):
    o_ref = out_ref.at[0]
    seq = o_ref.shape[0]
    ntile = seq // A_QSUB

    def place(dil, t):
        seq_l = seq // dil
        nsub = seq_l // A_QSUB
        r = t // nsub
        sub = t % nsub
        row0 = pl.multiple_of(sub * A_QSUB, A_QSUB)
        start = pl.multiple_of(jnp.clip(row0 - A_RADIUS, 0, seq_l - A_KWIN), A_RADIUS)
        case = jnp.where(sub == 0, 0, jnp.where(sub == nsub - 1, 2, 1))
        if dil == 1:
            rows = pl.ds(row0, A_QSUB)
        else:
            rows = pl.ds(row0 * dil + r, A_QSUB, stride=dil)
        return r, row0, start, case, rows

    first = len(A_GROUPS) - 1
    for group, refs in reversed(tuple(enumerate(((q0, k0, v0), (q1, k1, v1), (q2, k2, v2))))):
        dil = A_DILS[group]

        def step(it, carry, group=group, refs=refs, dil=dil):
            q_ref, k_ref, v_ref = refs
            n = A_TILES_PER_ITER
            places = [place(dil, it * n + u) for u in range(n)]
            scs, stats = {}, {}

            def score(u):
                r, row0, start, case, _ = places[u]
                scs[u] = (_dot_nt(q_ref[0, r, pl.ds(row0, A_QSUB), :],
                                  k_ref[0, r, pl.ds(start, A_KWIN), :])
                          + bias_ref[0, 3 * group + case])

            def softmax_values(u):
                r, _, start, _, _ = places[u]
                sc = scs.pop(u)
                m = jnp.max(sc, axis=-1, keepdims=True)
                p = jnp.exp2(sc - m)
                l = jnp.sum(p, axis=-1, keepdims=True)
                acc = jnp.dot(p.astype(BF16), v_ref[0, r, pl.ds(start, A_KWIN), :],
                              preferred_element_type=F32)
                stats[u] = (m, l, acc)

            def merge(u):
                rows = places[u][4]
                m, l, acc = stats.pop(u)
                if group == first:
                    m_scr[rows, :] = jnp.broadcast_to(m, (A_QSUB, LANES))
                    l_scr[rows, :] = jnp.broadcast_to(l, (A_QSUB, LANES))
                    o_ref[rows, :] = acc
                else:
                    m_old = m_scr[rows, :]
                    m_new = jnp.maximum(m_old, m)
                    a_old = jnp.exp2(m_old - m_new)
                    a_cur = jnp.exp2(m - m_new)
                    m_scr[rows, :] = m_new
                    l_scr[rows, :] = l_scr[rows, :] * a_old + l * a_cur
                    o_ref[rows, :] = o_ref[rows, :] * a_old + acc * a_cur

            for s in range(n + A_SCORE_AHEAD + A_MERGE_BEHIND):
                if s < n:
                    score(s)
                if 0 <= s - A_SCORE_AHEAD < n:
                    softmax_values(s - A_SCORE_AHEAD)
                if 0 <= s - A_SCORE_AHEAD - A_MERGE_BEHIND < n:
                    merge(s - A_SCORE_AHEAD - A_MERGE_BEHIND)
            return carry

        lax.fori_loop(0, ntile // A_TILES_PER_ITER, step, 0)

    o_ref[...] = o_ref[...] / l_scr[...]


def _alibi_bias_tables():
    n = len(A_GROUPS) * A_HEADS
    slopes = (2.0 ** (-ALIBI_MAX_EXP * np.arange(1, n + 1) / n)).astype(np.float32)
    slopes = slopes.reshape(len(A_GROUPS), A_HEADS)
    i = np.arange(A_QSUB)[:, None]
    j = np.arange(A_KWIN)[None, :]
    out = np.empty((A_HEADS, 3 * len(A_GROUPS), A_QSUB, A_KWIN), np.float32)
    for g, dil in enumerate(A_DILS):
        for case, shift in enumerate((0, -A_RADIUS, -2 * A_RADIUS)):
            delta = np.abs(j - i + shift)
            bias = (-slopes[g][:, None, None] * (delta * dil)[None]).astype(np.float32)
            bias = bias * np.float32(LOG2E)
            out[:, 3 * g + case] = np.where((delta <= A_RADIUS)[None], bias, np.float32(NEG_BIG))
    return out


def _a_attn(qkvs, seq):
    b = qkvs[0].shape[0]
    in_specs, args = [], []
    for qkv, dil in zip(qkvs, A_DILS):
        for n in range(3):
            in_specs.append(pl.BlockSpec((1, dil, seq // dil, A_HEAD_DIM),
                                         lambda bb, hh, n=n: (bb, 0, 0, n * A_HEADS + hh)))
            args.append(qkv)
    in_specs.append(pl.BlockSpec((1, 3 * len(A_GROUPS), A_QSUB, A_KWIN),
                                 lambda bb, hh: (hh, 0, 0, 0)))
    args.append(jnp.asarray(_alibi_bias_tables()))
    return pl.pallas_call(
        _a_attn_kernel,
        grid=(b, A_HEADS),
        in_specs=in_specs,
        out_specs=pl.BlockSpec((1, seq, A_HEAD_DIM), lambda bb, hh: (bb, 0, hh)),
        out_shape=jax.ShapeDtypeStruct((b, seq, A_GROUP_WIDTH), F32),
        scratch_shapes=[pltpu.VMEM((seq, LANES), F32), pltpu.VMEM((seq, LANES), F32)],
        compiler_params=_params("parallel", "arbitrary"),
        name="a_attn",
    )(*args)


def _dilated_layer(x, g, w_in_bf):
    qkv0, qkv1, qkv2, gate = _a_proj(x, g, w_in_bf)
    return _a_attn((qkv0, qkv1, qkv2), x.shape[1]), gate


def _rope(t, ca, cm, cp):
    return t * ca + pltpu.roll(t, 128 - B_ROPE // 2, 1) * cm + pltpu.roll(t, B_ROPE // 2, 1) * cp


def _b_proj_kernel(x_ref, o_ref, pgate_ref, wprev_ref, g_ref, w1_ref, qg_ref, wq_ref, kvg_ref,
                   wk_ref, wvt_ref, ca_ref, cm_ref, cp_ref,
                   xnew_ref, q_ref, k_ref, vt_ref, gate_ref):
    y = (o_ref[0] * _silu(pgate_ref[0])).astype(BF16)
    x = x_ref[0] + jnp.dot(y, wprev_ref[...], preferred_element_type=F32)
    xnew_ref[0] = x
    h = _rms(x, g_ref[...]).astype(BF16)
    proj = jnp.dot(h, w1_ref[...], preferred_element_type=F32)
    c_q = proj[:, :B_Q_LORA]
    c_kv = proj[:, B_Q_LORA:B_Q_LORA + B_KV_LORA]
    k_r = proj[:, B_Q_LORA + B_KV_LORA:B_Q_LORA + B_KV_LORA + B_QK_PAD]
    gate_ref[0] = proj[:, B_Q_LORA + B_KV_LORA + B_QK_PAD:]
    ca, cm, cp = ca_ref[...], cm_ref[...], cp_ref[...]

    qn = _rms(c_q, qg_ref[...]).astype(BF16)
    q = jnp.dot(qn, wq_ref[...], preferred_element_type=F32)
    for hh in range(B_HEADS):
        cs = slice(hh * B_QK_PAD, (hh + 1) * B_QK_PAD)
        q_ref[0, :, cs] = (_rope(q[:, cs], ca, cm, cp) * B_Q_PRESCALE).astype(BF16)

    kvn = _rms(c_kv, kvg_ref[...]).astype(BF16)
    kn = jnp.dot(kvn, wk_ref[...], preferred_element_type=F32)
    k_rot = _rope(k_r, ca, cm, cp)
    for hh in range(B_HEADS):
        cs = slice(hh * B_QK_PAD, (hh + 1) * B_QK_PAD)
        k_ref[0, :, cs] = (kn[:, cs] + k_rot).astype(BF16)
    vt_ref[0] = _dot_nt(wvt_ref[...], kvn).astype(BF16)


def _rope_coeffs(s):
    half = B_ROPE // 2
    inv_freq = (ROPE_THETA ** (-np.arange(0, B_ROPE, 2) / B_ROPE)).astype(np.float32)
    ang = jnp.arange(s, dtype=F32)[:, None] * jnp.asarray(inv_freq)[None, :]
    cos, sin = jnp.cos(ang), jnp.sin(ang)
    ones = jnp.ones((s, B_NOPE), F32)
    zn = jnp.zeros((s, B_NOPE), F32)
    zh = jnp.zeros((s, half), F32)
    zt = jnp.zeros((s, B_QK_PAD - B_NOPE - B_ROPE), F32)
    ca = jnp.concatenate([ones, cos, cos, zt], axis=1)
    cm = jnp.concatenate([zn, -sin, zh, zt], axis=1)
    cp = jnp.concatenate([zn, zh, sin, zt], axis=1)
    return ca, cm, cp


def _b_proj(x, o_prev, gate_prev, w_out_prev, g, w1, qg, wq, kvg, wk, wvt, coeffs, tm=512):
    b, s, _ = x.shape
    n1 = w1.shape[1]
    const = lambda bb, ii: (0, 0)
    tok = lambda bb, ii: (bb, ii, 0)
    qk_w = B_HEADS * B_QK_PAD
    v_w = B_HEADS * B_V
    return pl.pallas_call(
        _b_proj_kernel,
        grid=(b, s // tm),
        in_specs=[
            pl.BlockSpec((1, tm, D_MODEL), tok),
            pl.BlockSpec((1, tm, A_GROUP_WIDTH), tok),
            pl.BlockSpec((1, tm, A_GROUP_WIDTH), tok),
            pl.BlockSpec((A_GROUP_WIDTH, D_MODEL), const),
            pl.BlockSpec((1, D_MODEL), const),
            pl.BlockSpec((D_MODEL, n1), const),
            pl.BlockSpec((1, B_Q_LORA), const),
            pl.BlockSpec((B_Q_LORA, qk_w), const),
            pl.BlockSpec((1, B_KV_LORA), const),
            pl.BlockSpec((B_KV_LORA, qk_w), const),
            pl.BlockSpec((v_w, B_KV_LORA), const),
            pl.BlockSpec((tm, B_QK_PAD), lambda bb, ii: (ii, 0)),
            pl.BlockSpec((tm, B_QK_PAD), lambda bb, ii: (ii, 0)),
            pl.BlockSpec((tm, B_QK_PAD), lambda bb, ii: (ii, 0)),
        ],
        out_specs=[
            pl.BlockSpec((1, tm, D_MODEL), tok),
            pl.BlockSpec((1, tm, qk_w), tok),
            pl.BlockSpec((1, tm, qk_w), tok),
            pl.BlockSpec((1, v_w, tm), lambda bb, ii: (bb, 0, ii)),
            pl.BlockSpec((1, tm, v_w), tok),
        ],
        out_shape=[
            jax.ShapeDtypeStruct((b, s, D_MODEL), F32),
            jax.ShapeDtypeStruct((b, s, qk_w), BF16),
            jax.ShapeDtypeStruct((b, s, qk_w), BF16),
            jax.ShapeDtypeStruct((b, v_w, s), BF16),
            jax.ShapeDtypeStruct((b, s, v_w), F32),
        ],
        compiler_params=_params("parallel", "parallel"),
        name="b_proj",
    )(x, o_prev, gate_prev, w_out_prev, g, w1, qg, wq, kvg, wk, wvt, *coeffs)


def _mla_attn_kernel(q_ref, k_ref, vt_ref, gate_ref, y_ref, st_scr, *, qb, tk, span):
    s = k_ref.shape[1]
    nq = s // qb
    nspan = s // (span * tk)

    def keys_of(c):
        return pl.ds(pl.multiple_of(c * tk, tk), tk)

    def queries_of(qi):
        return pl.ds(pl.multiple_of(qi * qb, qb), qb)

    def scores(qi, sp, slot):
        keys = pl.ds(pl.multiple_of(sp * (span * tk), span * tk), span * tk)
        for hd in range(2):
            cs = slice(hd * B_QK_PAD, (hd + 1) * B_QK_PAD)
            st_scr[slot, hd] = _dot_nt(k_ref[0, keys, cs], q_ref[0, queries_of(qi), cs])

    ones_rows = jnp.ones((B_ONES_ROWS, tk), BF16)

    def update(c, slot, part, carry):
        new = []
        for hd in range(2):
            m, acc = carry[hd]
            vrows = slice(hd * B_V, (hd + 1) * B_V)
            st = st_scr[slot, hd, part * tk:(part + 1) * tk, :]
            mn = jnp.maximum(m, jnp.max(st, axis=0, keepdims=True))
            alpha = jnp.exp2(m - mn)
            p = jnp.exp2(st - mn).astype(BF16)
            v1 = jnp.concatenate([vt_ref[0, vrows, keys_of(c)], ones_rows], axis=0)
            acc = acc * alpha + jnp.dot(v1, p, preferred_element_type=F32)
            new.append((mn, acc))
        return tuple(new)

    def qblock(qi, _):
        init = (jnp.full((1, qb), NEG_BIG, F32), jnp.zeros((B_V + B_ONES_ROWS, qb), F32))
        carry = (init, init)
        for sp in range(nspan):
            if sp + 1 < nspan:
                scores(qi, sp + 1, (sp + 1) % 2)
            else:
                scores(jnp.minimum(qi + 1, nq - 1), 0, (sp + 1) % 2)
            for part in range(span):
                carry = update(sp * span + part, sp % 2, part, carry)
        outs = [acc[:B_V] / acc[B_V:B_V + 1] for _, acc in carry]
        o = jnp.concatenate(outs, axis=0).T
        rows = queries_of(qi)
        y_ref[0, rows, :] = (o * _silu(gate_ref[0, rows, :])).astype(BF16)
        return 0

    scores(0, 0, 0)
    lax.fori_loop(0, nq, qblock, 0)


def _mla_attn(q, k, vt, gate, qb=256, tk=256, span=2):
    b, s, _ = q.shape
    assert (s // (span * tk)) % 2 == 0
    rows = lambda bb, hp: (bb, 0, hp)
    return pl.pallas_call(
        functools.partial(_mla_attn_kernel, qb=qb, tk=tk, span=span),
        grid=(b, B_HEADS // 2),
        in_specs=[
            pl.BlockSpec((1, s, 2 * B_QK_PAD), rows),
            pl.BlockSpec((1, s, 2 * B_QK_PAD), rows),
            pl.BlockSpec((1, 2 * B_V, s), lambda bb, hp: (bb, hp, 0)),
            pl.BlockSpec((1, s, 2 * B_V), rows),
        ],
        out_specs=pl.BlockSpec((1, s, 2 * B_V), rows),
        out_shape=jax.ShapeDtypeStruct((b, s, B_HEADS * B_V), BF16),
        scratch_shapes=[pltpu.VMEM((2, 2, span * tk, qb), F32)],
        compiler_params=_params("parallel", "arbitrary"),
        name="b_attn",
    )(q, k, vt, gate)


def _out_proj_kernel(x_ref, y_ref, w_ref, fg_ref, o_ref, *, final_norm):
    xn = x_ref[...] + jnp.dot(y_ref[...], w_ref[...], preferred_element_type=F32)
    if final_norm:
        xn = _rms(xn, fg_ref[...])
    o_ref[...] = xn


def _out_proj(x2, y2, w_bf, fg, final_norm, tm=1024):
    t = x2.shape[0]
    tok = pl.BlockSpec((tm, D_MODEL), lambda i: (i, 0))
    return pl.pallas_call(
        functools.partial(_out_proj_kernel, final_norm=final_norm),
        grid=(t // tm,),
        in_specs=[tok, tok,
                  pl.BlockSpec((D_MODEL, D_MODEL), lambda i: (0, 0)),
                  pl.BlockSpec((1, D_MODEL), lambda i: (0, 0))],
        out_specs=tok,
        out_shape=jax.ShapeDtypeStruct((t, D_MODEL), F32),
        compiler_params=_params("parallel"),
        name="out_proj",
    )(x2, y2, w_bf, fg)


def _mla_layer(x, o_prev, gate_prev, w_out_prev, g, wts, coeffs, fg, final_norm):
    b, s, _ = x.shape
    w1, qg, wq, kvg, wk, wvt, w_out = wts
    x, q, k, vt, gate = _b_proj(x, o_prev, gate_prev, w_out_prev, g, w1, qg, wq, kvg, wk, wvt,
                                coeffs)
    y = _mla_attn(q, k, vt, gate)
    out = _out_proj(x.reshape(b * s, D_MODEL), y.reshape(b * s, D_MODEL), w_out, fg, final_norm)
    return out.reshape(b, s, D_MODEL)


def _prep_mla_weights(w_in, q_g, w_q_up, kv_g, w_kv_up, w_out):
    c0, c1, c2 = B_Q_LORA, B_Q_LORA + B_KV_LORA, B_Q_LORA + B_KV_LORA + B_ROPE
    pad_r = B_QK_PAD - B_NOPE - B_ROPE
    w_kr = jnp.pad(w_in[:, c1:c2], ((0, 0), (B_NOPE, pad_r)))
    w1 = jnp.concatenate([w_in[:, :c1], w_kr, w_in[:, c2:]], axis=1).astype(BF16)
    wq = w_q_up.reshape(B_Q_LORA, B_HEADS, B_NOPE + B_ROPE)
    wq = jnp.pad(wq, ((0, 0), (0, 0), (0, pad_r))).reshape(B_Q_LORA, B_HEADS * B_QK_PAD)
    wkv = w_kv_up.reshape(B_KV_LORA, B_HEADS, B_NOPE + B_V)
    wk = jnp.pad(wkv[:, :, :B_NOPE], ((0, 0), (0, 0), (0, B_QK_PAD - B_NOPE)))
    wk = wk.reshape(B_KV_LORA, B_HEADS * B_QK_PAD)
    wvt = wkv[:, :, B_NOPE:].reshape(B_KV_LORA, B_HEADS * B_V).T
    return (w1, q_g.reshape(1, -1), wq.astype(BF16), kv_g.reshape(1, -1),
            wk.astype(BF16), wvt.astype(BF16), w_out.astype(BF16))


def _trunk(x, norm_g, a_wts, b_wts, coeffs, fg):
    for i in range(DEPTH):
        g = norm_g[i].reshape(1, D_MODEL)
        j = i // 2
        if i % 2 == 0:
            w_in_a, w_out_a = a_wts[j]
            o_a, gate_a = _dilated_layer(x, g, w_in_a)
        else:
            x = _mla_layer(x, o_a, gate_a, w_out_a, g, b_wts[j], coeffs, fg,
                           final_norm=(i == DEPTH - 1))
    return x


def kernel(x_prompt, x_sample, norm_g, a_w_in, a_w_out, b_w_in, b_q_norm_g, b_w_q_up,
           b_kv_norm_g, b_w_kv_up, b_w_out, final_norm_g):
    a_wts = [(a_w_in[j].astype(BF16), a_w_out[j].astype(BF16)) for j in range(a_w_in.shape[0])]
    b_wts = [_prep_mla_weights(b_w_in[j], b_q_norm_g[j], b_w_q_up[j], b_kv_norm_g[j],
                               b_w_kv_up[j], b_w_out[j]) for j in range(b_w_in.shape[0])]
    fg = final_norm_g.reshape(1, D_MODEL)
    outs = []
    for x in (x_prompt, x_sample):
        coeffs = _rope_coeffs(x.shape[1])
        outs.append(_trunk(x, norm_g, a_wts, b_wts, coeffs, fg))
    return tuple(outs)
```

```python
import functools

import numpy as np
import jax
import jax.numpy as jnp
from jax import lax
from jax.experimental import pallas as pl
from jax.experimental.pallas import tpu as pltpu

F32 = jnp.float32
BF16 = jnp.bfloat16

D_MODEL = 1024
DEPTH = 4
RMS_EPS = 1e-6
NEG_BIG = -1e30
LOG2E = float(np.log2(np.e))
LANES = 128

A_GROUPS = ((128, 1), (512, 4), (2048, 16))
A_DILS = tuple(d for _, d in A_GROUPS)
A_HEADS = 8
A_HEAD_DIM = 128
A_GROUP_WIDTH = A_HEADS * A_HEAD_DIM
A_NQKV = 3 * len(A_GROUPS)
A_PREP_STEP = A_NQKV - 2
ALIBI_MAX_EXP = 8.0
A_RADIUS = 64
A_QSUB = 128
A_KWIN = A_QSUB + 2 * A_RADIUS
A_TILES_PER_ITER = 8
A_SCORE_AHEAD = 2
A_MERGE_BEHIND = 2
A_Q_PRESCALE = float(LOG2E * A_HEAD_DIM ** -0.5)

B_HEADS = 16
B_NOPE = 64
B_ROPE = 32
B_V = 64
B_Q_LORA = 384
B_KV_LORA = 256
B_QK_PAD = 128
B_ONES_ROWS = 16
ROPE_THETA = 10000.0
B_Q_PRESCALE = float(LOG2E * (B_NOPE + B_ROPE) ** -0.5)

VMEM_LIMIT = 56 * 1024 * 1024


def _params(*sem):
    return pltpu.CompilerParams(dimension_semantics=sem, vmem_limit_bytes=VMEM_LIMIT)


def _rms(xf, g):
    ms = jnp.mean(xf * xf, axis=-1, keepdims=True)
    return xf * lax.rsqrt(ms + RMS_EPS) * g


def _silu(g):
    return g * (1.0 / (1.0 + jnp.exp(-g)))


def _dot_nt(a, b):
    return lax.dot_general(a, b, (((1,), (1,)), ((), ())), preferred_element_type=F32)


def _a_proj_kernel(x_ref, g_ref, w_ref, qkv0_ref, qkv1_ref, qkv2_ref, gate_ref,
                   h_scr, h1_scr, h2_scr, hf_scr, hfm_scr):
    blk = pl.program_id(0) * pl.num_programs(1) + pl.program_id(1)
    j = pl.program_id(2)
    slot = blk % 2
    tm = x_ref.shape[1]
    nslab = D_MODEL // LANES

    def normalise(dst_slot, part=0, nparts=1):
        n = tm // nparts
        rows = slice(part * n, (part + 1) * n)
        h = _rms(x_ref[0, rows, :], g_ref[...])
        h_scr[dst_slot, rows, :] = h.astype(BF16)
        for c in range(nslab):
            hf_scr[c * tm + part * n:c * tm + (part + 1) * n, :] = h[:, c * LANES:(c + 1) * LANES]

    ratio = A_DILS[2] // A_DILS[1]
    assert A_DILS[1] == ratio

    def permute_mid(residues=range(A_DILS[1])):
        n = tm // A_DILS[1]
        for r in residues:
            for c in range(nslab):
                piece = hf_scr[pl.ds(c * tm + r, n, stride=A_DILS[1]), :]
                hfm_scr[c * tm + r * n:c * tm + (r + 1) * n, :] = piece
                h1_scr[r * n:(r + 1) * n, c * LANES:(c + 1) * LANES] = piece.astype(BF16)

    def permute_wide(residues=range(A_DILS[2])):
        n_mid = tm // A_DILS[1]
        n = tm // A_DILS[2]
        for r in residues:
            r_mid, a = r % A_DILS[1], r // A_DILS[1]
            for c in range(nslab):
                piece = hfm_scr[pl.ds(c * tm + r_mid * n_mid + a, n, stride=ratio), :]
                h2_scr[r * n:(r + 1) * n, c * LANES:(c + 1) * LANES] = piece.astype(BF16)

    def project(lhs):
        acc = jnp.dot(lhs, w_ref[...], preferred_element_type=F32)
        return acc * jnp.where(j % 3 == 0, A_Q_PRESCALE, 1.0)

    nchunk = 4
    cw = A_GROUP_WIDTH // nchunk

    def project_group2(chunk=None):
        cols = slice(None) if chunk is None else slice(chunk * cw, (chunk + 1) * cw)
        acc = jnp.dot(h2_scr[...], w_ref[:, cols], preferred_element_type=F32)
        acc = (acc * jnp.where(j % 3 == 0, A_Q_PRESCALE, 1.0)).astype(BF16)
        n = tm // A_DILS[2]
        for r in range(A_DILS[2]):
            qkv2_ref[0, r, :, cols] = acc[r * n:(r + 1) * n]

    @pl.when((blk == 0) & (j == 0))
    def _():
        normalise(0)
        permute_mid()
        permute_wide()

    @pl.when(j < 3)
    def _():
        qkv0_ref[0, 0] = project(h_scr[slot]).astype(BF16)

    @pl.when((j >= 3) & (j < 6))
    def _():
        acc = project(h1_scr[...]).astype(BF16)
        n = tm // A_DILS[1]
        for r in range(A_DILS[1]):
            qkv1_ref[0, r] = acc[r * n:(r + 1) * n]

    @pl.when(j == 6)
    def _():
        project_group2()

    @pl.when(j == A_PREP_STEP)
    def _():
        for q in range(nchunk):
            project_group2(q)
            normalise(1 - slot, q, nchunk)

    @pl.when(j == A_PREP_STEP + 1)
    def _():
        per = A_DILS[1] // nchunk
        for q in range(nchunk):
            project_group2(q)
            permute_mid(range(q * per, (q + 1) * per))

    @pl.when(j == A_NQKV)
    def _():
        per = A_DILS[2] // nchunk
        for q in range(nchunk):
            gate_ref[0, :, q * cw:(q + 1) * cw] = jnp.dot(
                h_scr[slot], w_ref[:, q * cw:(q + 1) * cw], preferred_element_type=F32)
            permute_wide(range(q * per, (q + 1) * per))


def _a_proj(x, g, w_bf, tm=1024):
    b, s, _ = x.shape
    gw = A_GROUP_WIDTH

    def qkv_spec(group):
        dil = A_DILS[group]
        return pl.BlockSpec((1, dil, tm // dil, gw),
                            lambda bb, ii, jj: (bb, 0, ii, jnp.clip(jj - 3 * group, 0, 2)))

    nblk = s // tm

    def x_index(bb, ii, jj):
        nxt = jnp.minimum(bb * nblk + ii + (jj >= A_PREP_STEP).astype(jnp.int32), b * nblk - 1)
        return (nxt // nblk, nxt % nblk, 0)

    return pl.pallas_call(
        _a_proj_kernel,
        grid=(b, nblk, A_NQKV + 1),
        in_specs=[
            pl.BlockSpec((1, tm, D_MODEL), x_index),
            pl.BlockSpec((1, D_MODEL), lambda bb, ii, jj: (0, 0)),
            pl.BlockSpec((D_MODEL, gw), lambda bb, ii, jj: (0, jj)),
        ],
        out_specs=[qkv_spec(0), qkv_spec(1), qkv_spec(2),
                   pl.BlockSpec((1, tm, gw), lambda bb, ii, jj: (bb, ii, 0))],
        out_shape=[jax.ShapeDtypeStruct((b, d, s // d, 3 * gw), BF16) for d in A_DILS]
        + [jax.ShapeDtypeStruct((b, s, gw), F32)],
        scratch_shapes=[pltpu.VMEM((2, tm, D_MODEL), BF16),
                        pltpu.VMEM((tm, D_MODEL), BF16),
                        pltpu.VMEM((tm, D_MODEL), BF16),
                        pltpu.VMEM((D_MODEL // LANES * tm, LANES), F32),
                        pltpu.VMEM((D_MODEL // LANES * tm, LANES), F32)],
        compiler_params=_params("arbitrary", "arbitrary", "arbitrary"),
        name="a_proj",
    )(x, g, w_bf)


def _a_attn_kernel(q0, k0, v0, q1, k1, v1, q2, k2, v2, bias_ref, out_ref, m_scr, l_scr):
    o_ref = out_ref.at[0]
    seq = o_ref.shape[0]
    ntile = seq // A_QSUB
    ones_cols = jnp.ones((A_KWIN, LANES), BF16)

    def place(dil, t):
        seq_l = seq // dil
        nsub = seq_l // A_QSUB
        r = t // nsub
        sub = t % nsub
        row0 = pl.multiple_of(sub * A_QSUB, A_QSUB)
        start = pl.multiple_of(jnp.clip(row0 - A_RADIUS, 0, seq_l - A_KWIN), A_RADIUS)
        case = jnp.where(sub == 0, 0, jnp.where(sub == nsub - 1, 2, 1))
        if dil == 1:
            rows = pl.ds(row0, A_QSUB)
        else:
            rows = pl.ds(row0 * dil + r, A_QSUB, stride=dil)
        return r, row0, start, case, rows

    first = len(A_GROUPS) - 1
    for group, refs in reversed(tuple(enumerate(((q0, k0, v0), (q1, k1, v1), (q2, k2, v2))))):
        dil = A_DILS[group]

        def step(it, carry, group=group, refs=refs, dil=dil):
            q_ref, k_ref, v_ref = refs
            n = A_TILES_PER_ITER
            places = [place(dil, it * n + u) for u in range(n)]
            scs, stats = {}, {}

            def score(u):
                r, row0, start, case, _ = places[u]
                scs[u] = (_dot_nt(q_ref[0, r, pl.ds(row0, A_QSUB), :],
                                  k_ref[0, r, pl.ds(start, A_KWIN), :])
                          + bias_ref[0, 3 * group + case])

            def softmax_values(u):
                r, _, start, _, _ = places[u]
                sc = scs.pop(u)
                m = jnp.max(sc, axis=-1, keepdims=True)
                p = jnp.exp2(sc - m).astype(BF16)
                v1 = jnp.concatenate([v_ref[0, r, pl.ds(start, A_KWIN), :], ones_cols], axis=1)
                acc = jnp.dot(p, v1, preferred_element_type=F32)
                stats[u] = (m, acc[:, A_HEAD_DIM:], acc[:, :A_HEAD_DIM])

            def merge(u):
                rows = places[u][4]
                m, l, acc = stats.pop(u)
                if group == first:
                    m_scr[rows, :] = jnp.broadcast_to(m, (A_QSUB, LANES))
                    l_scr[rows, :] = l
                    o_ref[rows, :] = acc
                else:
                    m_old = m_scr[rows, :]
                    m_new = jnp.maximum(m_old, m)
                    a_old = jnp.exp2(m_old - m_new)
                    a_cur = jnp.exp2(m - m_new)
                    m_scr[rows, :] = m_new
                    l_scr[rows, :] = l_scr[rows, :] * a_old + l * a_cur
                    o_ref[rows, :] = o_ref[rows, :] * a_old + acc * a_cur

            for s in range(n + A_SCORE_AHEAD + A_MERGE_BEHIND):
                if s < n:
                    score(s)
                if 0 <= s - A_SCORE_AHEAD < n:
                    softmax_values(s - A_SCORE_AHEAD)
                if 0 <= s - A_SCORE_AHEAD - A_MERGE_BEHIND < n:
                    merge(s - A_SCORE_AHEAD - A_MERGE_BEHIND)
            return carry

        lax.fori_loop(0, ntile // A_TILES_PER_ITER, step, 0)

    o_ref[...] = o_ref[...] / l_scr[...]


def _alibi_bias_tables():
    n = len(A_GROUPS) * A_HEADS
    slopes = (2.0 ** (-ALIBI_MAX_EXP * np.arange(1, n + 1) / n)).astype(np.float32)
    slopes = slopes.reshape(len(A_GROUPS), A_HEADS)
    i = np.arange(A_QSUB)[:, None]
    j = np.arange(A_KWIN)[None, :]
    out = np.empty((A_HEADS, 3 * len(A_GROUPS), A_QSUB, A_KWIN), np.float32)
    for g, dil in enumerate(A_DILS):
        for case, shift in enumerate((0, -A_RADIUS, -2 * A_RADIUS)):
            delta = np.abs(j - i + shift)
            bias = (-slopes[g][:, None, None] * (delta * dil)[None]).astype(np.float32)
            bias = bias * np.float32(LOG2E)
            out[:, 3 * g + case] = np.where((delta <= A_RADIUS)[None], bias, np.float32(NEG_BIG))
    return out


def _a_attn(qkvs, seq):
    b = qkvs[0].shape[0]
    in_specs, args = [], []
    for qkv, dil in zip(qkvs, A_DILS):
        for n in range(3):
            in_specs.append(pl.BlockSpec((1, dil, seq // dil, A_HEAD_DIM),
                                         lambda bb, hh, n=n: (bb, 0, 0, n * A_HEADS + hh)))
            args.append(qkv)
    in_specs.append(pl.BlockSpec((1, 3 * len(A_GROUPS), A_QSUB, A_KWIN),
                                 lambda bb, hh: (hh, 0, 0, 0)))
    args.append(jnp.asarray(_alibi_bias_tables()))
    return pl.pallas_call(
        _a_attn_kernel,
        grid=(b, A_HEADS),
        in_specs=in_specs,
        out_specs=pl.BlockSpec((1, seq, A_HEAD_DIM), lambda bb, hh: (bb, 0, hh)),
        out_shape=jax.ShapeDtypeStruct((b, seq, A_GROUP_WIDTH), F32),
        scratch_shapes=[pltpu.VMEM((seq, LANES), F32), pltpu.VMEM((seq, LANES), F32)],
        compiler_params=_params("parallel", "arbitrary"),
        name="a_attn",
    )(*args)


def _dilated_layer(x, g, w_in_bf):
    qkv0, qkv1, qkv2, gate = _a_proj(x, g, w_in_bf)
    return _a_attn((qkv0, qkv1, qkv2), x.shape[1]), gate


def _rope(t, ca, cm, cp):
    return t * ca + pltpu.roll(t, 128 - B_ROPE // 2, 1) * cm + pltpu.roll(t, B_ROPE // 2, 1) * cp


def _b_proj_kernel(x_ref, o_ref, pgate_ref, wprev_ref, g_ref, w1_ref, qg_ref, wq_ref, kvg_ref,
                   wk_ref, wvt_ref, ca_ref, cm_ref, cp_ref,
                   xnew_ref, q_ref, k_ref, vt_ref, gate_ref):
    y = (o_ref[0] * _silu(pgate_ref[0])).astype(BF16)
    x = x_ref[0] + jnp.dot(y, wprev_ref[...], preferred_element_type=F32)
    xnew_ref[0] = x
    h = _rms(x, g_ref[...]).astype(BF16)
    proj = jnp.dot(h, w1_ref[...], preferred_element_type=F32)
    c_q = proj[:, :B_Q_LORA]
    c_kv = proj[:, B_Q_LORA:B_Q_LORA + B_KV_LORA]
    k_r = proj[:, B_Q_LORA + B_KV_LORA:B_Q_LORA + B_KV_LORA + B_QK_PAD]
    gate_ref[0] = proj[:, B_Q_LORA + B_KV_LORA + B_QK_PAD:]
    ca, cm, cp = ca_ref[...], cm_ref[...], cp_ref[...]

    qn = _rms(c_q, qg_ref[...]).astype(BF16)
    q = jnp.dot(qn, wq_ref[...], preferred_element_type=F32)
    for hh in range(B_HEADS):
        cs = slice(hh * B_QK_PAD, (hh + 1) * B_QK_PAD)
        q_ref[0, :, cs] = (_rope(q[:, cs], ca, cm, cp) * B_Q_PRESCALE).astype(BF16)

    kvn = _rms(c_kv, kvg_ref[...]).astype(BF16)
    kn = jnp.dot(kvn, wk_ref[...], preferred_element_type=F32)
    k_rot = _rope(k_r, ca, cm, cp)
    for hh in range(B_HEADS):
        cs = slice(hh * B_QK_PAD, (hh + 1) * B_QK_PAD)
        k_ref[0, :, cs] = (kn[:, cs] + k_rot).astype(BF16)
    vt_ref[0] = _dot_nt(wvt_ref[...], kvn).astype(BF16)


def _rope_coeffs(s):
    half = B_ROPE // 2
    inv_freq = (ROPE_THETA ** (-np.arange(0, B_ROPE, 2) / B_ROPE)).astype(np.float32)
    ang = jnp.arange(s, dtype=F32)[:, None] * jnp.asarray(inv_freq)[None, :]
    cos, sin = jnp.cos(ang), jnp.sin(ang)
    ones = jnp.ones((s, B_NOPE), F32)
    zn = jnp.zeros((s, B_NOPE), F32)
    zh = jnp.zeros((s, half), F32)
    zt = jnp.zeros((s, B_QK_PAD - B_NOPE - B_ROPE), F32)
    ca = jnp.concatenate([ones, cos, cos, zt], axis=1)
    cm = jnp.concatenate([zn, -sin, zh, zt], axis=1)
    cp = jnp.concatenate([zn, zh, sin, zt], axis=1)
    return ca, cm, cp


def _b_proj(x, o_prev, gate_prev, w_out_prev, g, w1, qg, wq, kvg, wk, wvt, coeffs, tm=512):
    b, s, _ = x.shape
    n1 = w1.shape[1]
    const = lambda bb, ii: (0, 0)
    tok = lambda bb, ii: (bb, ii, 0)
    qk_w = B_HEADS * B_QK_PAD
    v_w = B_HEADS * B_V
    return pl.pallas_call(
        _b_proj_kernel,
        grid=(b, s // tm),
        in_specs=[
            pl.BlockSpec((1, tm, D_MODEL), tok),
            pl.BlockSpec((1, tm, A_GROUP_WIDTH), tok),
            pl.BlockSpec((1, tm, A_GROUP_WIDTH), tok),
            pl.BlockSpec((A_GROUP_WIDTH, D_MODEL), const),
            pl.BlockSpec((1, D_MODEL), const),
            pl.BlockSpec((D_MODEL, n1), const),
            pl.BlockSpec((1, B_Q_LORA), const),
            pl.BlockSpec((B_Q_LORA, qk_w), const),
            pl.BlockSpec((1, B_KV_LORA), const),
            pl.BlockSpec((B_KV_LORA, qk_w), const),
            pl.BlockSpec((v_w, B_KV_LORA), const),
            pl.BlockSpec((tm, B_QK_PAD), lambda bb, ii: (ii, 0)),
            pl.BlockSpec((tm, B_QK_PAD), lambda bb, ii: (ii, 0)),
            pl.BlockSpec((tm, B_QK_PAD), lambda bb, ii: (ii, 0)),
        ],
        out_specs=[
            pl.BlockSpec((1, tm, D_MODEL), tok),
            pl.BlockSpec((1, tm, qk_w), tok),
            pl.BlockSpec((1, tm, qk_w), tok),
            pl.BlockSpec((1, v_w, tm), lambda bb, ii: (bb, 0, ii)),
            pl.BlockSpec((1, tm, v_w), tok),
        ],
        out_shape=[
            jax.ShapeDtypeStruct((b, s, D_MODEL), F32),
            jax.ShapeDtypeStruct((b, s, qk_w), BF16),
            jax.ShapeDtypeStruct((b, s, qk_w), BF16),
            jax.ShapeDtypeStruct((b, v_w, s), BF16),
            jax.ShapeDtypeStruct((b, s, v_w), F32),
        ],
        compiler_params=_params("parallel", "parallel"),
        name="b_proj",
    )(x, o_prev, gate_prev, w_out_prev, g, w1, qg, wq, kvg, wk, wvt, *coeffs)


def _mla_attn_kernel(q_ref, k_ref, vt_ref, gate_ref, y_ref, st_scr, *, qb, tk, span):
    s = k_ref.shape[1]
    nq = s // qb
    nspan = s // (span * tk)

    def keys_of(c):
        return pl.ds(pl.multiple_of(c * tk, tk), tk)

    def queries_of(qi):
        return pl.ds(pl.multiple_of(qi * qb, qb), qb)

    def scores(qi, sp, slot):
        keys = pl.ds(pl.multiple_of(sp * (span * tk), span * tk), span * tk)
        for hd in range(2):
            cs = slice(hd * B_QK_PAD, (hd + 1) * B_QK_PAD)
            st_scr[slot, hd] = _dot_nt(k_ref[0, keys, cs], q_ref[0, queries_of(qi), cs])

    ones_rows = jnp.ones((B_ONES_ROWS, tk), BF16)

    def update(c, slot, part, carry):
        new = []
        for hd in range(2):
            m, acc = carry[hd]
            vrows = slice(hd * B_V, (hd + 1) * B_V)
            st = st_scr[slot, hd, part * tk:(part + 1) * tk, :]
            mn = jnp.maximum(m, jnp.max(st, axis=0, keepdims=True))
            alpha = jnp.exp2(m - mn)
            p = jnp.exp2(st - mn).astype(BF16)
            v1 = jnp.concatenate([vt_ref[0, vrows, keys_of(c)], ones_rows], axis=0)
            acc = acc * alpha + jnp.dot(v1, p, preferred_element_type=F32)
            new.append((mn, acc))
        return tuple(new)

    def qblock(qi, _):
        init = (jnp.full((1, qb), NEG_BIG, F32), jnp.zeros((B_V + B_ONES_ROWS, qb), F32))
        carry = (init, init)
        for sp in range(nspan):
            if sp + 1 < nspan:
                scores(qi, sp + 1, (sp + 1) % 2)
            else:
                scores(jnp.minimum(qi + 1, nq - 1), 0, (sp + 1) % 2)
            for part in range(span):
                carry = update(sp * span + part, sp % 2, part, carry)
        outs = [acc[:B_V] / acc[B_V:B_V + 1] for _, acc in carry]
        o = jnp.concatenate(outs, axis=0).T
        rows = queries_of(qi)
        y_ref[0, rows, :] = (o * _silu(gate_ref[0, rows, :])).astype(BF16)
        return 0

    scores(0, 0, 0)
    lax.fori_loop(0, nq, qblock, 0)


def _mla_attn(q, k, vt, gate, qb=256, tk=256, span=2):
    b, s, _ = q.shape
    assert (s // (span * tk)) % 2 == 0
    rows = lambda bb, hp: (bb, 0, hp)
    return pl.pallas_call(
        functools.partial(_mla_attn_kernel, qb=qb, tk=tk, span=span),
        grid=(b, B_HEADS // 2),
        in_specs=[
            pl.BlockSpec((1, s, 2 * B_QK_PAD), rows),
            pl.BlockSpec((1, s, 2 * B_QK_PAD), rows),
            pl.BlockSpec((1, 2 * B_V, s), lambda bb, hp: (bb, hp, 0)),
            pl.BlockSpec((1, s, 2 * B_V), rows),
        ],
        out_specs=pl.BlockSpec((1, s, 2 * B_V), rows),
        out_shape=jax.ShapeDtypeStruct((b, s, B_HEADS * B_V), BF16),
        scratch_shapes=[pltpu.VMEM((2, 2, span * tk, qb), F32)],
        compiler_params=_params("parallel", "arbitrary"),
        name="b_attn",
    )(q, k, vt, gate)


def _out_proj_kernel(x_ref, y_ref, w_ref, fg_ref, o_ref, *, final_norm):
    xn = x_ref[...] + jnp.dot(y_ref[...], w_ref[...], preferred_element_type=F32)
    if final_norm:
        xn = _rms(xn, fg_ref[...])
    o_ref[...] = xn


def _out_proj(x2, y2, w_bf, fg, final_norm, tm=1024):
    t = x2.shape[0]
    tok = pl.BlockSpec((tm, D_MODEL), lambda i: (i, 0))
    return pl.pallas_call(
        functools.partial(_out_proj_kernel, final_norm=final_norm),
        grid=(t // tm,),
        in_specs=[tok, tok,
                  pl.BlockSpec((D_MODEL, D_MODEL), lambda i: (0, 0)),
                  pl.BlockSpec((1, D_MODEL), lambda i: (0, 0))],
        out_specs=tok,
        out_shape=jax.ShapeDtypeStruct((t, D_MODEL), F32),
        compiler_params=_params("parallel"),
        name="out_proj",
    )(x2, y2, w_bf, fg)


def _mla_layer(x, o_prev, gate_prev, w_out_prev, g, wts, coeffs, fg, final_norm):
    b, s, _ = x.shape
    w1, qg, wq, kvg, wk, wvt, w_out = wts
    x, q, k, vt, gate = _b_proj(x, o_prev, gate_prev, w_out_prev, g, w1, qg, wq, kvg, wk, wvt,
                                coeffs)
    y = _mla_attn(q, k, vt, gate)
    out = _out_proj(x.reshape(b * s, D_MODEL), y.reshape(b * s, D_MODEL), w_out, fg, final_norm)
    return out.reshape(b, s, D_MODEL)


def _prep_mla_weights(w_in, q_g, w_q_up, kv_g, w_kv_up, w_out):
    c0, c1, c2 = B_Q_LORA, B_Q_LORA + B_KV_LORA, B_Q_LORA + B_KV_LORA + B_ROPE
    pad_r = B_QK_PAD - B_NOPE - B_ROPE
    w_kr = jnp.pad(w_in[:, c1:c2], ((0, 0), (B_NOPE, pad_r)))
    w1 = jnp.concatenate([w_in[:, :c1], w_kr, w_in[:, c2:]], axis=1).astype(BF16)
    wq = w_q_up.reshape(B_Q_LORA, B_HEADS, B_NOPE + B_ROPE)
    wq = jnp.pad(wq, ((0, 0), (0, 0), (0, pad_r))).reshape(B_Q_LORA, B_HEADS * B_QK_PAD)
    wkv = w_kv_up.reshape(B_KV_LORA, B_HEADS, B_NOPE + B_V)
    wk = jnp.pad(wkv[:, :, :B_NOPE], ((0, 0), (0, 0), (0, B_QK_PAD - B_NOPE)))
    wk = wk.reshape(B_KV_LORA, B_HEADS * B_QK_PAD)
    wvt = wkv[:, :, B_NOPE:].reshape(B_KV_LORA, B_HEADS * B_V).T
    return (w1, q_g.reshape(1, -1), wq.astype(BF16), kv_g.reshape(1, -1),
            wk.astype(BF16), wvt.astype(BF16), w_out.astype(BF16))


def _trunk(x, norm_g, a_wts, b_wts, coeffs, fg):
    for i in range(DEPTH):
        g = norm_g[i].reshape(1, D_MODEL)
        j = i // 2
        if i % 2 == 0:
            w_in_a, w_out_a = a_wts[j]
            o_a, gate_a = _dilated_layer(x, g, w_in_a)
        else:
            x = _mla_layer(x, o_a, gate_a, w_out_a, g, b_wts[j], coeffs, fg,
                           final_norm=(i == DEPTH - 1))
    return x


def kernel(x_prompt, x_sample, norm_g, a_w_in, a_w_out, b_w_in, b_q_norm_g, b_w_q_up,
           b_kv_norm_g, b_w_kv_up, b_w_out, final_norm_g):
    a_wts = [(a_w_in[j].astype(BF16), a_w_out[j].astype(BF16)) for j in range(a_w_in.shape[0])]
    b_wts = [_prep_mla_weights(b_w_in[j], b_q_norm_g[j], b_w_q_up[j], b_kv_norm_g[j],
                               b_w_kv_up[j], b_w_out[j]) for j in range(b_w_in.shape[0])]
    fg = final_norm_g.reshape(1, D_MODEL)
    outs = []
    for x in (x_prompt, x_sample):
        coeffs = _rope_coeffs(x.shape[1])
        outs.append(_trunk(x, norm_g, a_wts, b_wts, coeffs, fg))
    return tuple(outs)
```

```python
import functools

import numpy as np
import jax
import jax.numpy as jnp
from jax import lax
from jax.experimental import pallas as pl
from jax.experimental.pallas import tpu as pltpu

F32 = jnp.float32
BF16 = jnp.bfloat16

D_MODEL = 1024
DEPTH = 4
RMS_EPS = 1e-6
NEG_BIG = -1e30
LOG2E = float(np.log2(np.e))
LANES = 128

A_GROUPS = ((128, 1), (512, 4), (2048, 16))
A_DILS = tuple(d for _, d in A_GROUPS)
A_HEADS = 8
A_HEAD_DIM = 128
A_GROUP_WIDTH = A_HEADS * A_HEAD_DIM
A_NQKV = 3 * len(A_GROUPS)
A_PREP_STEP = A_NQKV - 2
ALIBI_MAX_EXP = 8.0
A_RADIUS = 64
A_QSUB = 128
A_KWIN = A_QSUB + 2 * A_RADIUS
A_TILES_PER_ITER = 8
A_SCORE_AHEAD = 2
A_MERGE_BEHIND = 2
A_Q_PRESCALE = float(LOG2E * A_HEAD_DIM ** -0.5)

B_HEADS = 16
B_NOPE = 64
B_ROPE = 32
B_V = 64
B_Q_LORA = 384
B_KV_LORA = 256
B_QK_PAD = 128
B_ONES_ROWS = 16
ROPE_THETA = 10000.0
B_Q_PRESCALE = float(LOG2E * (B_NOPE + B_ROPE) ** -0.5)

VMEM_LIMIT = 56 * 1024 * 1024


def _params(*sem):
    return pltpu.CompilerParams(dimension_semantics=sem, vmem_limit_bytes=VMEM_LIMIT)


def _rms(xf, g):
    ms = jnp.mean(xf * xf, axis=-1, keepdims=True)
    return xf * lax.rsqrt(ms + RMS_EPS) * g


def _silu(g):
    return g * (1.0 / (1.0 + jnp.exp(-g)))


def _dot_nt(a, b):
    return lax.dot_general(a, b, (((1,), (1,)), ((), ())), preferred_element_type=F32)


def _a_proj_kernel(x_ref, g_ref, w_ref, qkv0_ref, qkv1_ref, qkv2_ref, gate_ref,
                   h_scr, h1_scr, h2_scr, hf_scr, hfm_scr):
    blk = pl.program_id(0) * pl.num_programs(1) + pl.program_id(1)
    j = pl.program_id(2)
    slot = blk % 2
    tm = x_ref.shape[1]
    nslab = D_MODEL // LANES

    def normalise(dst_slot, part=0, nparts=1):
        n = tm // nparts
        rows = slice(part * n, (part + 1) * n)
        h = _rms(x_ref[0, rows, :], g_ref[...])
        h_scr[dst_slot, rows, :] = h.astype(BF16)
        for c in range(nslab):
            hf_scr[c * tm + part * n:c * tm + (part + 1) * n, :] = h[:, c * LANES:(c + 1) * LANES]

    ratio = A_DILS[2] // A_DILS[1]
    assert A_DILS[1] == ratio

    def permute_mid(residues=range(A_DILS[1])):
        n = tm // A_DILS[1]
        for r in residues:
            for c in range(nslab):
                piece = hf_scr[pl.ds(c * tm + r, n, stride=A_DILS[1]), :]
                hfm_scr[c * tm + r * n:c * tm + (r + 1) * n, :] = piece
                h1_scr[r * n:(r + 1) * n, c * LANES:(c + 1) * LANES] = piece.astype(BF16)

    def permute_wide(residues=range(A_DILS[2])):
        n_mid = tm // A_DILS[1]
        n = tm // A_DILS[2]
        for r in residues:
            r_mid, a = r % A_DILS[1], r // A_DILS[1]
            for c in range(nslab):
                piece = hfm_scr[pl.ds(c * tm + r_mid * n_mid + a, n, stride=ratio), :]
                h2_scr[r * n:(r + 1) * n, c * LANES:(c + 1) * LANES] = piece.astype(BF16)

    def project(lhs):
        acc = jnp.dot(lhs, w_ref[0], preferred_element_type=F32)
        return acc * jnp.where(j % 3 == 0, A_Q_PRESCALE, 1.0)

    nchunk = 4
    cw = A_GROUP_WIDTH // nchunk

    def project_group2(chunk=None):
        cols = slice(None) if chunk is None else slice(chunk * cw, (chunk + 1) * cw)
        acc = jnp.dot(h2_scr[...], w_ref[0, :, cols], preferred_element_type=F32)
        acc = (acc * jnp.where(j % 3 == 0, A_Q_PRESCALE, 1.0)).astype(BF16)
        n = tm // A_DILS[2]
        for r in range(A_DILS[2]):
            qkv2_ref[0, 0, r, :, cols] = acc[r * n:(r + 1) * n]

    @pl.when((blk == 0) & (j == 0))
    def _():
        normalise(0)
        permute_mid()
        permute_wide()

    @pl.when(j < 3)
    def _():
        qkv0_ref[0, 0, 0] = project(h_scr[slot]).astype(BF16)

    @pl.when((j >= 3) & (j < 6))
    def _():
        acc = project(h1_scr[...]).astype(BF16)
        n = tm // A_DILS[1]
        for r in range(A_DILS[1]):
            qkv1_ref[0, 0, r] = acc[r * n:(r + 1) * n]

    @pl.when(j == 6)
    def _():
        project_group2()

    @pl.when(j == A_PREP_STEP)
    def _():
        for q in range(nchunk):
            project_group2(q)
            normalise(1 - slot, q, nchunk)

    @pl.when(j == A_PREP_STEP + 1)
    def _():
        per = A_DILS[1] // nchunk
        for q in range(nchunk):
            project_group2(q)
            permute_mid(range(q * per, (q + 1) * per))

    @pl.when(j == A_NQKV)
    def _():
        per = A_DILS[2] // nchunk
        for q in range(nchunk):
            gate_ref[0, :, q * cw:(q + 1) * cw] = jnp.dot(
                h_scr[slot], w_ref[0, :, q * cw:(q + 1) * cw], preferred_element_type=F32)
            permute_wide(range(q * per, (q + 1) * per))


def _a_proj(x, g, w_bf, tm=1024):
    b, s, _ = x.shape
    gw = A_GROUP_WIDTH

    def qkv_spec(group):
        dil = A_DILS[group]
        return pl.BlockSpec((1, 1, dil, tm // dil, gw),
                            lambda bb, ii, jj: (bb, jnp.clip(jj - 3 * group, 0, 2), 0, ii, 0))

    nblk = s // tm

    def x_index(bb, ii, jj):
        nxt = jnp.minimum(bb * nblk + ii + (jj >= A_PREP_STEP).astype(jnp.int32), b * nblk - 1)
        return (nxt // nblk, nxt % nblk, 0)

    return pl.pallas_call(
        _a_proj_kernel,
        grid=(b, nblk, A_NQKV + 1),
        in_specs=[
            pl.BlockSpec((1, tm, D_MODEL), x_index),
            pl.BlockSpec((1, D_MODEL), lambda bb, ii, jj: (0, 0)),
            pl.BlockSpec((1, D_MODEL, gw), lambda bb, ii, jj: (jj, 0, 0)),
        ],
        out_specs=[qkv_spec(0), qkv_spec(1), qkv_spec(2),
                   pl.BlockSpec((1, tm, gw), lambda bb, ii, jj: (bb, ii, 0))],
        out_shape=[jax.ShapeDtypeStruct((b, 3, d, s // d, gw), BF16) for d in A_DILS]
        + [jax.ShapeDtypeStruct((b, s, gw), F32)],
        scratch_shapes=[pltpu.VMEM((2, tm, D_MODEL), BF16),
                        pltpu.VMEM((tm, D_MODEL), BF16),
                        pltpu.VMEM((tm, D_MODEL), BF16),
                        pltpu.VMEM((D_MODEL // LANES * tm, LANES), F32),
                        pltpu.VMEM((D_MODEL // LANES * tm, LANES), F32)],
        compiler_params=_params("arbitrary", "arbitrary", "arbitrary"),
        name="a_proj",
    )(x, g, w_bf)


def _a_attn_kernel(q0, k0, v0, q1, k1, v1, q2, k2, v2, bias_ref, out_ref, m_scr, l_scr):
    o_ref = out_ref.at[0]
    seq = o_ref.shape[0]
    ntile = seq // A_QSUB
    ones_cols = jnp.ones((A_KWIN, LANES), BF16)

    def place(dil, t):
        seq_l = seq // dil
        nsub = seq_l // A_QSUB
        r = t // nsub
        sub = t % nsub
        row0 = pl.multiple_of(sub * A_QSUB, A_QSUB)
        start = pl.multiple_of(jnp.clip(row0 - A_RADIUS, 0, seq_l - A_KWIN), A_RADIUS)
        case = jnp.where(sub == 0, 0, jnp.where(sub == nsub - 1, 2, 1))
        if dil == 1:
            rows = pl.ds(row0, A_QSUB)
        else:
            rows = pl.ds(row0 * dil + r, A_QSUB, stride=dil)
        return r, row0, start, case, rows

    first = len(A_GROUPS) - 1
    for group, refs in reversed(tuple(enumerate(((q0, k0, v0), (q1, k1, v1), (q2, k2, v2))))):
        dil = A_DILS[group]

        def step(it, carry, group=group, refs=refs, dil=dil):
            q_ref, k_ref, v_ref = refs
            n = A_TILES_PER_ITER
            places = [place(dil, it * n + u) for u in range(n)]
            scs, stats = {}, {}

            def score(u):
                r, row0, start, case, _ = places[u]
                scs[u] = (_dot_nt(q_ref[0, 0, r, pl.ds(row0, A_QSUB), :],
                                  k_ref[0, 0, r, pl.ds(start, A_KWIN), :])
                          + bias_ref[0, 3 * group + case])

            def softmax_values(u):
                r, _, start, _, _ = places[u]
                sc = scs.pop(u)
                m = jnp.max(sc, axis=-1, keepdims=True)
                p = jnp.exp2(sc - m).astype(BF16)
                v1 = jnp.concatenate([v_ref[0, 0, r, pl.ds(start, A_KWIN), :], ones_cols], axis=1)
                acc = jnp.dot(p, v1, preferred_element_type=F32)
                stats[u] = (m, acc[:, A_HEAD_DIM:], acc[:, :A_HEAD_DIM])

            def merge(u):
                rows = places[u][4]
                m, l, acc = stats.pop(u)
                if group == first:
                    m_scr[rows, :] = jnp.broadcast_to(m, (A_QSUB, LANES))
                    l_scr[rows, :] = l
                    o_ref[rows, :] = acc
                else:
                    m_old = m_scr[rows, :]
                    m_new = jnp.maximum(m_old, m)
                    a_old = jnp.exp2(m_old - m_new)
                    a_cur = jnp.exp2(m - m_new)
                    m_scr[rows, :] = m_new
                    l_scr[rows, :] = l_scr[rows, :] * a_old + l * a_cur
                    o_ref[rows, :] = o_ref[rows, :] * a_old + acc * a_cur

            for s in range(n + A_SCORE_AHEAD + A_MERGE_BEHIND):
                if s < n:
                    score(s)
                if 0 <= s - A_SCORE_AHEAD < n:
                    softmax_values(s - A_SCORE_AHEAD)
                if 0 <= s - A_SCORE_AHEAD - A_MERGE_BEHIND < n:
                    merge(s - A_SCORE_AHEAD - A_MERGE_BEHIND)
            return carry

        lax.fori_loop(0, ntile // A_TILES_PER_ITER, step, 0)

    o_ref[...] = o_ref[...] / l_scr[...]


def _alibi_bias_tables():
    n = len(A_GROUPS) * A_HEADS
    slopes = (2.0 ** (-ALIBI_MAX_EXP * np.arange(1, n + 1) / n)).astype(np.float32)
    slopes = slopes.reshape(len(A_GROUPS), A_HEADS)
    i = np.arange(A_QSUB)[:, None]
    j = np.arange(A_KWIN)[None, :]
    out = np.empty((A_HEADS, 3 * len(A_GROUPS), A_QSUB, A_KWIN), np.float32)
    for g, dil in enumerate(A_DILS):
        for case, shift in enumerate((0, -A_RADIUS, -2 * A_RADIUS)):
            delta = np.abs(j - i + shift)
            bias = (-slopes[g][:, None, None] * (delta * dil)[None]).astype(np.float32)
            bias = bias * np.float32(LOG2E)
            out[:, 3 * g + case] = np.where((delta <= A_RADIUS)[None], bias, np.float32(NEG_BIG))
    return out


def _a_attn(qkvs, seq):
    b = qkvs[0].shape[0]
    in_specs, args = [], []
    for qkv, dil in zip(qkvs, A_DILS):
        for n in range(3):
            in_specs.append(pl.BlockSpec((1, 1, dil, seq // dil, A_HEAD_DIM),
                                         lambda bb, hh, n=n: (bb, n, 0, 0, hh)))
            args.append(qkv)
    in_specs.append(pl.BlockSpec((1, 3 * len(A_GROUPS), A_QSUB, A_KWIN),
                                 lambda bb, hh: (hh, 0, 0, 0)))
    args.append(jnp.asarray(_alibi_bias_tables()))
    return pl.pallas_call(
        _a_attn_kernel,
        grid=(b, A_HEADS),
        in_specs=in_specs,
        out_specs=pl.BlockSpec((1, seq, A_HEAD_DIM), lambda bb, hh: (bb, 0, hh)),
        out_shape=jax.ShapeDtypeStruct((b, seq, A_GROUP_WIDTH), F32),
        scratch_shapes=[pltpu.VMEM((seq, LANES), F32), pltpu.VMEM((seq, LANES), F32)],
        compiler_params=_params("parallel", "arbitrary"),
        name="a_attn",
    )(*args)


def _dilated_layer(x, g, w_in_bf):
    qkv0, qkv1, qkv2, gate = _a_proj(x, g, w_in_bf)
    return _a_attn((qkv0, qkv1, qkv2), x.shape[1]), gate


def _rope(t, ca, cm, cp):
    return t * ca + pltpu.roll(t, 128 - B_ROPE // 2, 1) * cm + pltpu.roll(t, B_ROPE // 2, 1) * cp


def _b_proj_kernel(x_ref, o_ref, pgate_ref, wprev_ref, g_ref, w1_ref, qg_ref, wq_ref, kvg_ref,
                   wk_ref, wvt_ref, ca_ref, cm_ref, cp_ref,
                   xnew_ref, q_ref, k_ref, vt_ref, gate_ref):
    y = (o_ref[0] * _silu(pgate_ref[0])).astype(BF16)
    x = x_ref[0] + jnp.dot(y, wprev_ref[...], preferred_element_type=F32)
    xnew_ref[0] = x
    h = _rms(x, g_ref[...]).astype(BF16)
    proj = jnp.dot(h, w1_ref[...], preferred_element_type=F32)
    c_q = proj[:, :B_Q_LORA]
    c_kv = proj[:, B_Q_LORA:B_Q_LORA + B_KV_LORA]
    k_r = proj[:, B_Q_LORA + B_KV_LORA:B_Q_LORA + B_KV_LORA + B_QK_PAD]
    gate_ref[0] = proj[:, B_Q_LORA + B_KV_LORA + B_QK_PAD:]
    ca, cm, cp = ca_ref[...], cm_ref[...], cp_ref[...]

    qn = _rms(c_q, qg_ref[...]).astype(BF16)
    q = jnp.dot(qn, wq_ref[...], preferred_element_type=F32)
    for hh in range(B_HEADS):
        cs = slice(hh * B_QK_PAD, (hh + 1) * B_QK_PAD)
        q_ref[0, :, cs] = (_rope(q[:, cs], ca, cm, cp) * B_Q_PRESCALE).astype(BF16)

    kvn = _rms(c_kv, kvg_ref[...]).astype(BF16)
    kn = jnp.dot(kvn, wk_ref[...], preferred_element_type=F32)
    k_rot = _rope(k_r, ca, cm, cp)
    for hh in range(B_HEADS):
        cs = slice(hh * B_QK_PAD, (hh + 1) * B_QK_PAD)
        k_ref[0, :, cs] = (kn[:, cs] + k_rot).astype(BF16)
    vt_ref[0] = _dot_nt(wvt_ref[...], kvn).astype(BF16)


def _rope_coeffs(s):
    half = B_ROPE // 2
    inv_freq = (ROPE_THETA ** (-np.arange(0, B_ROPE, 2) / B_ROPE)).astype(np.float32)
    ang = jnp.arange(s, dtype=F32)[:, None] * jnp.asarray(inv_freq)[None, :]
    cos, sin = jnp.cos(ang), jnp.sin(ang)
    ones = jnp.ones((s, B_NOPE), F32)
    zn = jnp.zeros((s, B_NOPE), F32)
    zh = jnp.zeros((s, half), F32)
    zt = jnp.zeros((s, B_QK_PAD - B_NOPE - B_ROPE), F32)
    ca = jnp.concatenate([ones, cos, cos, zt], axis=1)
    cm = jnp.concatenate([zn, -sin, zh, zt], axis=1)
    cp = jnp.concatenate([zn, zh, sin, zt], axis=1)
    return ca, cm, cp


def _b_proj(x, o_prev, gate_prev, w_out_prev, g, w1, qg, wq, kvg, wk, wvt, coeffs, tm=512):
    b, s, _ = x.shape
    n1 = w1.shape[1]
    const = lambda bb, ii: (0, 0)
    tok = lambda bb, ii: (bb, ii, 0)
    qk_w = B_HEADS * B_QK_PAD
    v_w = B_HEADS * B_V
    return pl.pallas_call(
        _b_proj_kernel,
        grid=(b, s // tm),
        in_specs=[
            pl.BlockSpec((1, tm, D_MODEL), tok),
            pl.BlockSpec((1, tm, A_GROUP_WIDTH), tok),
            pl.BlockSpec((1, tm, A_GROUP_WIDTH), tok),
            pl.BlockSpec((A_GROUP_WIDTH, D_MODEL), const),
            pl.BlockSpec((1, D_MODEL), const),
            pl.BlockSpec((D_MODEL, n1), const),
            pl.BlockSpec((1, B_Q_LORA), const),
            pl.BlockSpec((B_Q_LORA, qk_w), const),
            pl.BlockSpec((1, B_KV_LORA), const),
            pl.BlockSpec((B_KV_LORA, qk_w), const),
            pl.BlockSpec((v_w, B_KV_LORA), const),
            pl.BlockSpec((tm, B_QK_PAD), lambda bb, ii: (ii, 0)),
            pl.BlockSpec((tm, B_QK_PAD), lambda bb, ii: (ii, 0)),
            pl.BlockSpec((tm, B_QK_PAD), lambda bb, ii: (ii, 0)),
        ],
        out_specs=[
            pl.BlockSpec((1, tm, D_MODEL), tok),
            pl.BlockSpec((1, tm, qk_w), tok),
            pl.BlockSpec((1, tm, qk_w), tok),
            pl.BlockSpec((1, v_w, tm), lambda bb, ii: (bb, 0, ii)),
            pl.BlockSpec((1, tm, v_w), tok),
        ],
        out_shape=[
            jax.ShapeDtypeStruct((b, s, D_MODEL), F32),
            jax.ShapeDtypeStruct((b, s, qk_w), BF16),
            jax.ShapeDtypeStruct((b, s, qk_w), BF16),
            jax.ShapeDtypeStruct((b, v_w, s), BF16),
            jax.ShapeDtypeStruct((b, s, v_w), F32),
        ],
        compiler_params=_params("parallel", "parallel"),
        name="b_proj",
    )(x, o_prev, gate_prev, w_out_prev, g, w1, qg, wq, kvg, wk, wvt, *coeffs)


def _mla_attn_kernel(q_ref, k_ref, vt_ref, gate_ref, y_ref, st_scr, *, qb, tk, span):
    s = k_ref.shape[1]
    nq = s // qb
    nspan = s // (span * tk)

    def keys_of(c):
        return pl.ds(pl.multiple_of(c * tk, tk), tk)

    def queries_of(qi):
        return pl.ds(pl.multiple_of(qi * qb, qb), qb)

    def scores(qi, sp, slot):
        keys = pl.ds(pl.multiple_of(sp * (span * tk), span * tk), span * tk)
        for hd in range(2):
            cs = slice(hd * B_QK_PAD, (hd + 1) * B_QK_PAD)
            st_scr[slot, hd] = _dot_nt(k_ref[0, keys, cs], q_ref[0, queries_of(qi), cs])

    ones_rows = jnp.ones((B_ONES_ROWS, tk), BF16)

    def update(c, slot, part, carry):
        new = []
        for hd in range(2):
            m, acc = carry[hd]
            vrows = slice(hd * B_V, (hd + 1) * B_V)
            st = st_scr[slot, hd, part * tk:(part + 1) * tk, :]
            mn = jnp.maximum(m, jnp.max(st, axis=0, keepdims=True))
            alpha = jnp.exp2(m - mn)
            p = jnp.exp2(st - mn).astype(BF16)
            v1 = jnp.concatenate([vt_ref[0, vrows, keys_of(c)], ones_rows], axis=0)
            acc = acc * alpha + jnp.dot(v1, p, preferred_element_type=F32)
            new.append((mn, acc))
        return tuple(new)

    def qblock(qi, _):
        init = (jnp.full((1, qb), NEG_BIG, F32), jnp.zeros((B_V + B_ONES_ROWS, qb), F32))
        carry = (init, init)
        for sp in range(nspan):
            if sp + 1 < nspan:
                scores(qi, sp + 1, (sp + 1) % 2)
            else:
                scores(jnp.minimum(qi + 1, nq - 1), 0, (sp + 1) % 2)
            for part in range(span):
                carry = update(sp * span + part, sp % 2, part, carry)
        outs = [acc[:B_V] / acc[B_V:B_V + 1] for _, acc in carry]
        o = jnp.concatenate(outs, axis=0).T
        rows = queries_of(qi)
        y_ref[0, rows, :] = (o * _silu(gate_ref[0, rows, :])).astype(BF16)
        return 0

    scores(0, 0, 0)
    lax.fori_loop(0, nq, qblock, 0)


def _mla_attn(q, k, vt, gate, qb=256, tk=256, span=2):
    b, s, _ = q.shape
    assert (s // (span * tk)) % 2 == 0
    rows = lambda bb, hp: (bb, 0, hp)
    return pl.pallas_call(
        functools.partial(_mla_attn_kernel, qb=qb, tk=tk, span=span),
        grid=(b, B_HEADS // 2),
        in_specs=[
            pl.BlockSpec((1, s, 2 * B_QK_PAD), rows),
            pl.BlockSpec((1, s, 2 * B_QK_PAD), rows),
            pl.BlockSpec((1, 2 * B_V, s), lambda bb, hp: (bb, hp, 0)),
            pl.BlockSpec((1, s, 2 * B_V), rows),
        ],
        out_specs=pl.BlockSpec((1, s, 2 * B_V), rows),
        out_shape=jax.ShapeDtypeStruct((b, s, B_HEADS * B_V), BF16),
        scratch_shapes=[pltpu.VMEM((2, 2, span * tk, qb), F32)],
        compiler_params=_params("parallel", "arbitrary"),
        name="b_attn",
    )(q, k, vt, gate)


def _out_proj_kernel(x_ref, y_ref, w_ref, fg_ref, o_ref, *, final_norm):
    xn = x_ref[...] + jnp.dot(y_ref[...], w_ref[...], preferred_element_type=F32)
    if final_norm:
        xn = _rms(xn, fg_ref[...])
    o_ref[...] = xn


def _out_proj(x2, y2, w_bf, fg, final_norm, tm=1024):
    t = x2.shape[0]
    tok = pl.BlockSpec((tm, D_MODEL), lambda i: (i, 0))
    return pl.pallas_call(
        functools.partial(_out_proj_kernel, final_norm=final_norm),
        grid=(t // tm,),
        in_specs=[tok, tok,
                  pl.BlockSpec((D_MODEL, D_MODEL), lambda i: (0, 0)),
                  pl.BlockSpec((1, D_MODEL), lambda i: (0, 0))],
        out_specs=tok,
        out_shape=jax.ShapeDtypeStruct((t, D_MODEL), F32),
        compiler_params=_params("parallel"),
        name="out_proj",
    )(x2, y2, w_bf, fg)


def _mla_layer(x, o_prev, gate_prev, w_out_prev, g, wts, coeffs, fg, final_norm):
    b, s, _ = x.shape
    w1, qg, wq, kvg, wk, wvt, w_out = wts
    x, q, k, vt, gate = _b_proj(x, o_prev, gate_prev, w_out_prev, g, w1, qg, wq, kvg, wk, wvt,
                                coeffs)
    y = _mla_attn(q, k, vt, gate)
    out = _out_proj(x.reshape(b * s, D_MODEL), y.reshape(b * s, D_MODEL), w_out, fg, final_norm)
    return out.reshape(b, s, D_MODEL)


def _prep_mla_weights(w_in, q_g, w_q_up, kv_g, w_kv_up, w_out):
    c0, c1, c2 = B_Q_LORA, B_Q_LORA + B_KV_LORA, B_Q_LORA + B_KV_LORA + B_ROPE
    pad_r = B_QK_PAD - B_NOPE - B_ROPE
    w_kr = jnp.pad(w_in[:, c1:c2], ((0, 0), (B_NOPE, pad_r)))
    w1 = jnp.concatenate([w_in[:, :c1], w_kr, w_in[:, c2:]], axis=1).astype(BF16)
    wq = w_q_up.reshape(B_Q_LORA, B_HEADS, B_NOPE + B_ROPE)
    wq = jnp.pad(wq, ((0, 0), (0, 0), (0, pad_r))).reshape(B_Q_LORA, B_HEADS * B_QK_PAD)
    wkv = w_kv_up.reshape(B_KV_LORA, B_HEADS, B_NOPE + B_V)
    wk = jnp.pad(wkv[:, :, :B_NOPE], ((0, 0), (0, 0), (0, B_QK_PAD - B_NOPE)))
    wk = wk.reshape(B_KV_LORA, B_HEADS * B_QK_PAD)
    wvt = wkv[:, :, B_NOPE:].reshape(B_KV_LORA, B_HEADS * B_V).T
    return (w1, q_g.reshape(1, -1), wq.astype(BF16), kv_g.reshape(1, -1),
            wk.astype(BF16), wvt.astype(BF16), w_out.astype(BF16))


def _trunk(x, norm_g, a_wts, b_wts, coeffs, fg):
    for i in range(DEPTH):
        g = norm_g[i].reshape(1, D_MODEL)
        j = i // 2
        if i % 2 == 0:
            w_in_a, w_out_a = a_wts[j]
            o_a, gate_a = _dilated_layer(x, g, w_in_a)
        else:
            x = _mla_layer(x, o_a, gate_a, w_out_a, g, b_wts[j], coeffs, fg,
                           final_norm=(i == DEPTH - 1))
    return x


def kernel(x_prompt, x_sample, norm_g, a_w_in, a_w_out, b_w_in, b_q_norm_g, b_w_q_up,
           b_kv_norm_g, b_w_kv_up, b_w_out, final_norm_g):
    a_wts = [(a_w_in[j].astype(BF16).reshape(D_MODEL, A_NQKV + 1, A_GROUP_WIDTH).transpose(1, 0, 2),
              a_w_out[j].astype(BF16)) for j in range(a_w_in.shape[0])]
    b_wts = [_prep_mla_weights(b_w_in[j], b_q_norm_g[j], b_w_q_up[j], b_kv_norm_g[j],
                               b_w_kv_up[j], b_w_out[j]) for j in range(b_w_in.shape[0])]
    fg = final_norm_g.reshape(1, D_MODEL)
    outs = []
    for x in (x_prompt, x_sample):
        coeffs = _rope_coeffs(x.shape[1])
        outs.append(_trunk(x, norm_g, a_wts, b_wts, coeffs, fg))
    return tuple(outs)
```

```python
import functools

import numpy as np
import jax
import jax.numpy as jnp
from jax import lax
from jax.experimental import pallas as pl
from jax.experimental.pallas import tpu as pltpu

F32 = jnp.float32
BF16 = jnp.bfloat16

D_MODEL = 1024
DEPTH = 4
RMS_EPS = 1e-6
NEG_BIG = -1e30
LOG2E = float(np.log2(np.e))
LANES = 128

A_GROUPS = ((128, 1), (512, 4), (2048, 16))
A_DILS = tuple(d for _, d in A_GROUPS)
A_HEADS = 8
A_HEAD_DIM = 128
A_GROUP_WIDTH = A_HEADS * A_HEAD_DIM
A_NQKV = 3 * len(A_GROUPS)
A_PREP_STEP = A_NQKV - 2
ALIBI_MAX_EXP = 8.0
A_RADIUS = 64
A_QSUB = 128
A_KWIN = A_QSUB + 2 * A_RADIUS
A_TILES_PER_ITER = 8
A_SCORE_AHEAD = 2
A_MERGE_BEHIND = 2
A_Q_PRESCALE = float(LOG2E * A_HEAD_DIM ** -0.5)

B_HEADS = 16
B_NOPE = 64
B_ROPE = 32
B_V = 64
B_Q_LORA = 384
B_KV_LORA = 256
B_QK_PAD = 128
B_ONES_ROWS = 16
ROPE_THETA = 10000.0
B_Q_PRESCALE = float(LOG2E * (B_NOPE + B_ROPE) ** -0.5)

VMEM_LIMIT = 56 * 1024 * 1024


def _params(*sem):
    return pltpu.CompilerParams(dimension_semantics=sem, vmem_limit_bytes=VMEM_LIMIT)


def _rms(xf, g):
    ms = jnp.mean(xf * xf, axis=-1, keepdims=True)
    return xf * lax.rsqrt(ms + RMS_EPS) * g


def _silu(g):
    return g * (1.0 / (1.0 + jnp.exp(-g)))


def _dot_nt(a, b):
    return lax.dot_general(a, b, (((1,), (1,)), ((), ())), preferred_element_type=F32)


def _a_proj_kernel(x_ref, g_ref, w_ref, qkv0_ref, qkv1_ref, qkv2_ref, gate_ref,
                   h_scr, h1_scr, h2_scr, hf_scr, hfm_scr):
    blk = pl.program_id(0) * pl.num_programs(1) + pl.program_id(1)
    j = pl.program_id(2)
    slot = blk % 2
    tm = x_ref.shape[1]
    nslab = D_MODEL // LANES

    def normalise(dst_slot, part=0, nparts=1):
        n = tm // nparts
        rows = slice(part * n, (part + 1) * n)
        h = _rms(x_ref[0, rows, :], g_ref[...])
        h_scr[dst_slot, rows, :] = h.astype(BF16)
        for c in range(nslab):
            hf_scr[c * tm + part * n:c * tm + (part + 1) * n, :] = h[:, c * LANES:(c + 1) * LANES]

    ratio = A_DILS[2] // A_DILS[1]
    assert A_DILS[1] == ratio

    def permute_mid(residues=range(A_DILS[1])):
        n = tm // A_DILS[1]
        for r in residues:
            for c in range(nslab):
                piece = hf_scr[pl.ds(c * tm + r, n, stride=A_DILS[1]), :]
                hfm_scr[c * tm + r * n:c * tm + (r + 1) * n, :] = piece
                h1_scr[r * n:(r + 1) * n, c * LANES:(c + 1) * LANES] = piece.astype(BF16)

    def permute_wide(residues=range(A_DILS[2])):
        n_mid = tm // A_DILS[1]
        n = tm // A_DILS[2]
        for r in residues:
            r_mid, a = r % A_DILS[1], r // A_DILS[1]
            for c in range(nslab):
                piece = hfm_scr[pl.ds(c * tm + r_mid * n_mid + a, n, stride=ratio), :]
                h2_scr[r * n:(r + 1) * n, c * LANES:(c + 1) * LANES] = piece.astype(BF16)

    def project(lhs):
        acc = jnp.dot(lhs, w_ref[0], preferred_element_type=F32)
        return acc * jnp.where(j % 3 == 0, A_Q_PRESCALE, 1.0)

    nchunk = 4
    cw = A_GROUP_WIDTH // nchunk

    def project_group2(chunk=None):
        cols = slice(None) if chunk is None else slice(chunk * cw, (chunk + 1) * cw)
        acc = jnp.dot(h2_scr[...], w_ref[0, :, cols], preferred_element_type=F32)
        acc = (acc * jnp.where(j % 3 == 0, A_Q_PRESCALE, 1.0)).astype(BF16)
        n = tm // A_DILS[2]
        for r in range(A_DILS[2]):
            qkv2_ref[0, 0, r, :, cols] = acc[r * n:(r + 1) * n]

    @pl.when((blk == 0) & (j == 0))
    def _():
        normalise(0)
        permute_mid()
        permute_wide()

    @pl.when(j < 3)
    def _():
        qkv0_ref[0, 0, 0] = project(h_scr[slot]).astype(BF16)

    @pl.when((j >= 3) & (j < 6))
    def _():
        acc = project(h1_scr[...]).astype(BF16)
        n = tm // A_DILS[1]
        for r in range(A_DILS[1]):
            qkv1_ref[0, 0, r] = acc[r * n:(r + 1) * n]

    @pl.when(j == 6)
    def _():
        project_group2()

    @pl.when(j == A_PREP_STEP)
    def _():
        for q in range(nchunk):
            project_group2(q)
            normalise(1 - slot, q, nchunk)

    @pl.when(j == A_PREP_STEP + 1)
    def _():
        per = A_DILS[1] // nchunk
        for q in range(nchunk):
            project_group2(q)
            permute_mid(range(q * per, (q + 1) * per))

    @pl.when(j == A_NQKV)
    def _():
        per = A_DILS[2] // nchunk
        for q in range(nchunk):
            gate_ref[0, :, q * cw:(q + 1) * cw] = jnp.dot(
                h_scr[slot], w_ref[0, :, q * cw:(q + 1) * cw], preferred_element_type=F32)
            permute_wide(range(q * per, (q + 1) * per))


def _a_proj(x, g, w_bf, tm=1024):
    b, s, _ = x.shape
    gw = A_GROUP_WIDTH

    def qkv_spec(group):
        dil = A_DILS[group]
        return pl.BlockSpec((1, 1, dil, tm // dil, gw),
                            lambda bb, ii, jj: (bb, jnp.clip(jj - 3 * group, 0, 2), 0, ii, 0))

    nblk = s // tm

    def x_index(bb, ii, jj):
        nxt = jnp.minimum(bb * nblk + ii + (jj >= A_PREP_STEP).astype(jnp.int32), b * nblk - 1)
        return (nxt // nblk, nxt % nblk, 0)

    return pl.pallas_call(
        _a_proj_kernel,
        grid=(b, nblk, A_NQKV + 1),
        in_specs=[
            pl.BlockSpec((1, tm, D_MODEL), x_index),
            pl.BlockSpec((1, D_MODEL), lambda bb, ii, jj: (0, 0)),
            pl.BlockSpec((1, D_MODEL, gw), lambda bb, ii, jj: (jj, 0, 0)),
        ],
        out_specs=[qkv_spec(0), qkv_spec(1), qkv_spec(2),
                   pl.BlockSpec((1, tm, gw), lambda bb, ii, jj: (bb, ii, 0))],
        out_shape=[jax.ShapeDtypeStruct((b, 3, d, s // d, gw), BF16) for d in A_DILS]
        + [jax.ShapeDtypeStruct((b, s, gw), F32)],
        scratch_shapes=[pltpu.VMEM((2, tm, D_MODEL), BF16),
                        pltpu.VMEM((tm, D_MODEL), BF16),
                        pltpu.VMEM((tm, D_MODEL), BF16),
                        pltpu.VMEM((D_MODEL // LANES * tm, LANES), F32),
                        pltpu.VMEM((D_MODEL // LANES * tm, LANES), F32)],
        compiler_params=_params("arbitrary", "arbitrary", "arbitrary"),
        name="a_proj",
    )(x, g, w_bf)


def _a_attn_kernel(q0, k0, v0, q1, k1, v1, q2, k2, v2, bias_ref, out_ref, m_scr, l_scr):
    o_ref = out_ref.at[0]
    seq = o_ref.shape[0]
    ntile = seq // A_QSUB
    ones_cols = jnp.ones((A_KWIN, LANES), BF16)

    def place(dil, t):
        seq_l = seq // dil
        nsub = seq_l // A_QSUB
        r = t // nsub
        sub = t % nsub
        row0 = pl.multiple_of(sub * A_QSUB, A_QSUB)
        start = pl.multiple_of(jnp.clip(row0 - A_RADIUS, 0, seq_l - A_KWIN), A_RADIUS)
        case = jnp.where(sub == 0, 0, jnp.where(sub == nsub - 1, 2, 1))
        if dil == 1:
            rows = pl.ds(row0, A_QSUB)
        else:
            rows = pl.ds(row0 * dil + r, A_QSUB, stride=dil)
        return r, row0, start, case, rows

    first = len(A_GROUPS) - 1
    for group, refs in reversed(tuple(enumerate(((q0, k0, v0), (q1, k1, v1), (q2, k2, v2))))):
        dil = A_DILS[group]

        def step(it, carry, group=group, refs=refs, dil=dil):
            q_ref, k_ref, v_ref = refs
            n = A_TILES_PER_ITER
            places = [place(dil, it * n + u) for u in range(n)]
            scs, stats = {}, {}

            def score(u):
                r, row0, start, case, _ = places[u]
                scs[u] = (_dot_nt(q_ref[0, 0, r, pl.ds(row0, A_QSUB), :],
                                  k_ref[0, 0, r, pl.ds(start, A_KWIN), :])
                          + bias_ref[0, 3 * group + case])

            def softmax_values(u):
                r, _, start, _, _ = places[u]
                sc = scs.pop(u)
                m = jnp.max(sc, axis=-1, keepdims=True)
                p = jnp.exp2(sc - m).astype(BF16)
                v1 = jnp.concatenate([v_ref[0, 0, r, pl.ds(start, A_KWIN), :], ones_cols], axis=1)
                acc = jnp.dot(p, v1, preferred_element_type=F32)
                stats[u] = (m, acc[:, A_HEAD_DIM:], acc[:, :A_HEAD_DIM])

            def merge(u):
                rows = places[u][4]
                m, l, acc = stats.pop(u)
                if group == first:
                    m_scr[rows, :] = jnp.broadcast_to(m, (A_QSUB, LANES))
                    l_scr[rows, :] = l
                    o_ref[rows, :] = acc
                else:
                    m_old = m_scr[rows, :]
                    m_new = jnp.maximum(m_old, m)
                    a_old = jnp.exp2(m_old - m_new)
                    a_cur = jnp.exp2(m - m_new)
                    m_scr[rows, :] = m_new
                    l_scr[rows, :] = l_scr[rows, :] * a_old + l * a_cur
                    o_ref[rows, :] = o_ref[rows, :] * a_old + acc * a_cur

            for s in range(n + A_SCORE_AHEAD + A_MERGE_BEHIND):
                if s < n:
                    score(s)
                if 0 <= s - A_SCORE_AHEAD < n:
                    softmax_values(s - A_SCORE_AHEAD)
                if 0 <= s - A_SCORE_AHEAD - A_MERGE_BEHIND < n:
                    merge(s - A_SCORE_AHEAD - A_MERGE_BEHIND)
            return carry

        lax.fori_loop(0, ntile // A_TILES_PER_ITER, step, 0)

    o_ref[...] = o_ref[...] / l_scr[...]


def _alibi_bias_tables():
    n = len(A_GROUPS) * A_HEADS
    slopes = (2.0 ** (-ALIBI_MAX_EXP * np.arange(1, n + 1) / n)).astype(np.float32)
    slopes = slopes.reshape(len(A_GROUPS), A_HEADS)
    i = np.arange(A_QSUB)[:, None]
    j = np.arange(A_KWIN)[None, :]
    out = np.empty((A_HEADS, 3 * len(A_GROUPS), A_QSUB, A_KWIN), np.float32)
    for g, dil in enumerate(A_DILS):
        for case, shift in enumerate((0, -A_RADIUS, -2 * A_RADIUS)):
            delta = np.abs(j - i + shift)
            bias = (-slopes[g][:, None, None] * (delta * dil)[None]).astype(np.float32)
            bias = bias * np.float32(LOG2E)
            out[:, 3 * g + case] = np.where((delta <= A_RADIUS)[None], bias, np.float32(NEG_BIG))
    return out


def _a_attn(qkvs, seq):
    b = qkvs[0].shape[0]
    in_specs, args = [], []
    for qkv, dil in zip(qkvs, A_DILS):
        for n in range(3):
            in_specs.append(pl.BlockSpec((1, 1, dil, seq // dil, A_HEAD_DIM),
                                         lambda bb, hh, n=n: (bb, n, 0, 0, hh)))
            args.append(qkv)
    in_specs.append(pl.BlockSpec((1, 3 * len(A_GROUPS), A_QSUB, A_KWIN),
                                 lambda bb, hh: (hh, 0, 0, 0)))
    args.append(jnp.asarray(_alibi_bias_tables()))
    return pl.pallas_call(
        _a_attn_kernel,
        grid=(b, A_HEADS),
        in_specs=in_specs,
        out_specs=pl.BlockSpec((1, seq, A_HEAD_DIM), lambda bb, hh: (bb, 0, hh)),
        out_shape=jax.ShapeDtypeStruct((b, seq, A_GROUP_WIDTH), F32),
        scratch_shapes=[pltpu.VMEM((seq, LANES), F32), pltpu.VMEM((seq, LANES), F32)],
        compiler_params=_params("parallel", "arbitrary"),
        name="a_attn",
    )(*args)


def _dilated_layer(x, g, w_in_bf):
    qkv0, qkv1, qkv2, gate = _a_proj(x, g, w_in_bf)
    return _a_attn((qkv0, qkv1, qkv2), x.shape[1]), gate


def _rope(t, ca, cm, cp):
    return t * ca + pltpu.roll(t, 128 - B_ROPE // 2, 1) * cm + pltpu.roll(t, B_ROPE // 2, 1) * cp


def _b_proj_kernel(x_ref, o_ref, pgate_ref, wprev_ref, g_ref, w1_ref, qg_ref, wq_ref, kvg_ref,
                   wk_ref, wvt_ref, ca_ref, cm_ref, cp_ref,
                   xnew_ref, q_ref, k_ref, vt_ref, gate_ref):
    y = (o_ref[0] * _silu(pgate_ref[0])).astype(BF16)
    x = x_ref[0] + jnp.dot(y, wprev_ref[...], preferred_element_type=F32)
    xnew_ref[0] = x
    h = _rms(x, g_ref[...]).astype(BF16)
    proj = jnp.dot(h, w1_ref[...], preferred_element_type=F32)
    c_q = proj[:, :B_Q_LORA]
    c_kv = proj[:, B_Q_LORA:B_Q_LORA + B_KV_LORA]
    k_r = proj[:, B_Q_LORA + B_KV_LORA:B_Q_LORA + B_KV_LORA + B_QK_PAD]
    gate_ref[0] = proj[:, B_Q_LORA + B_KV_LORA + B_QK_PAD:]
    ca, cm, cp = ca_ref[...], cm_ref[...], cp_ref[...]

    qn = _rms(c_q, qg_ref[...]).astype(BF16)
    q = jnp.dot(qn, wq_ref[...], preferred_element_type=F32)
    for hh in range(B_HEADS):
        cs = slice(hh * B_QK_PAD, (hh + 1) * B_QK_PAD)
        q_ref[0, :, cs] = (_rope(q[:, cs], ca, cm, cp) * B_Q_PRESCALE).astype(BF16)

    kvn = _rms(c_kv, kvg_ref[...]).astype(BF16)
    kn = jnp.dot(kvn, wk_ref[...], preferred_element_type=F32)
    k_rot = _rope(k_r, ca, cm, cp)
    for hh in range(B_HEADS):
        cs = slice(hh * B_QK_PAD, (hh + 1) * B_QK_PAD)
        k_ref[0, :, cs] = (kn[:, cs] + k_rot).astype(BF16)
    vt_ref[0] = _dot_nt(wvt_ref[...], kvn).astype(BF16)


def _rope_coeffs(s):
    half = B_ROPE // 2
    inv_freq = (ROPE_THETA ** (-np.arange(0, B_ROPE, 2) / B_ROPE)).astype(np.float32)
    ang = jnp.arange(s, dtype=F32)[:, None] * jnp.asarray(inv_freq)[None, :]
    cos, sin = jnp.cos(ang), jnp.sin(ang)
    ones = jnp.ones((s, B_NOPE), F32)
    zn = jnp.zeros((s, B_NOPE), F32)
    zh = jnp.zeros((s, half), F32)
    zt = jnp.zeros((s, B_QK_PAD - B_NOPE - B_ROPE), F32)
    ca = jnp.concatenate([ones, cos, cos, zt], axis=1)
    cm = jnp.concatenate([zn, -sin, zh, zt], axis=1)
    cp = jnp.concatenate([zn, zh, sin, zt], axis=1)
    return ca, cm, cp


def _b_proj(x, o_prev, gate_prev, w_out_prev, g, w1, qg, wq, kvg, wk, wvt, coeffs, tm=512):
    b, s, _ = x.shape
    n1 = w1.shape[1]
    const = lambda bb, ii: (0, 0)
    tok = lambda bb, ii: (bb, ii, 0)
    qk_w = B_HEADS * B_QK_PAD
    v_w = B_HEADS * B_V
    return pl.pallas_call(
        _b_proj_kernel,
        grid=(b, s // tm),
        in_specs=[
            pl.BlockSpec((1, tm, D_MODEL), tok),
            pl.BlockSpec((1, tm, A_GROUP_WIDTH), tok),
            pl.BlockSpec((1, tm, A_GROUP_WIDTH), tok),
            pl.BlockSpec((A_GROUP_WIDTH, D_MODEL), const),
            pl.BlockSpec((1, D_MODEL), const),
            pl.BlockSpec((D_MODEL, n1), const),
            pl.BlockSpec((1, B_Q_LORA), const),
            pl.BlockSpec((B_Q_LORA, qk_w), const),
            pl.BlockSpec((1, B_KV_LORA), const),
            pl.BlockSpec((B_KV_LORA, qk_w), const),
            pl.BlockSpec((v_w, B_KV_LORA), const),
            pl.BlockSpec((tm, B_QK_PAD), lambda bb, ii: (ii, 0)),
            pl.BlockSpec((tm, B_QK_PAD), lambda bb, ii: (ii, 0)),
            pl.BlockSpec((tm, B_QK_PAD), lambda bb, ii: (ii, 0)),
        ],
        out_specs=[
            pl.BlockSpec((1, tm, D_MODEL), tok),
            pl.BlockSpec((1, tm, qk_w), tok),
            pl.BlockSpec((1, tm, qk_w), tok),
            pl.BlockSpec((1, v_w, tm), lambda bb, ii: (bb, 0, ii)),
            pl.BlockSpec((1, tm, v_w), tok),
        ],
        out_shape=[
            jax.ShapeDtypeStruct((b, s, D_MODEL), F32),
            jax.ShapeDtypeStruct((b, s, qk_w), BF16),
            jax.ShapeDtypeStruct((b, s, qk_w), BF16),
            jax.ShapeDtypeStruct((b, v_w, s), BF16),
            jax.ShapeDtypeStruct((b, s, v_w), F32),
        ],
        compiler_params=_params("parallel", "parallel"),
        name="b_proj",
    )(x, o_prev, gate_prev, w_out_prev, g, w1, qg, wq, kvg, wk, wvt, *coeffs)


def _mla_attn_kernel(q_ref, k_ref, vt_ref, gate_ref, y_ref, st_scr, *, qb, tk, span):
    s = k_ref.shape[1]
    nq = s // qb
    nspan = s // (span * tk)

    def keys_of(c):
        return pl.ds(pl.multiple_of(c * tk, tk), tk)

    def queries_of(qi):
        return pl.ds(pl.multiple_of(qi * qb, qb), qb)

    def scores(qi, sp, slot):
        keys = pl.ds(pl.multiple_of(sp * (span * tk), span * tk), span * tk)
        for hd in range(2):
            cs = slice(hd * B_QK_PAD, (hd + 1) * B_QK_PAD)
            st_scr[slot, hd] = _dot_nt(k_ref[0, keys, cs], q_ref[0, queries_of(qi), cs])

    ones_rows = jnp.ones((B_ONES_ROWS, tk), BF16)

    def update(c, slot, part, carry):
        new = []
        for hd in range(2):
            m, acc = carry[hd]
            vrows = slice(hd * B_V, (hd + 1) * B_V)
            st = st_scr[slot, hd, part * tk:(part + 1) * tk, :]
            mn = jnp.maximum(m, jnp.max(st, axis=0, keepdims=True))
            alpha = jnp.exp2(m - mn)
            p = jnp.exp2(st - mn).astype(BF16)
            v1 = jnp.concatenate([vt_ref[0, vrows, keys_of(c)], ones_rows], axis=0)
            acc = acc * alpha + jnp.dot(v1, p, preferred_element_type=F32)
            new.append((mn, acc))
        return tuple(new)

    def qblock(qi, _):
        init = (jnp.full((1, qb), NEG_BIG, F32), jnp.zeros((B_V + B_ONES_ROWS, qb), F32))
        carry = (init, init)
        for sp in range(nspan):
            if sp + 1 < nspan:
                scores(qi, sp + 1, (sp + 1) % 2)
            else:
                scores(jnp.minimum(qi + 1, nq - 1), 0, (sp + 1) % 2)
            for part in range(span):
                carry = update(sp * span + part, sp % 2, part, carry)
        outs = [acc[:B_V] / acc[B_V:B_V + 1] for _, acc in carry]
        o = jnp.concatenate(outs, axis=0).T
        rows = queries_of(qi)
        y_ref[0, rows, :] = (o * _silu(gate_ref[0, rows, :])).astype(BF16)
        return 0

    scores(0, 0, 0)
    lax.fori_loop(0, nq, qblock, 0, unroll=4)


def _mla_attn(q, k, vt, gate, qb=256, tk=256, span=2):
    b, s, _ = q.shape
    assert (s // (span * tk)) % 2 == 0
    rows = lambda bb, hp: (bb, 0, hp)
    return pl.pallas_call(
        functools.partial(_mla_attn_kernel, qb=qb, tk=tk, span=span),
        grid=(b, B_HEADS // 2),
        in_specs=[
            pl.BlockSpec((1, s, 2 * B_QK_PAD), rows),
            pl.BlockSpec((1, s, 2 * B_QK_PAD), rows),
            pl.BlockSpec((1, 2 * B_V, s), lambda bb, hp: (bb, hp, 0)),
            pl.BlockSpec((1, s, 2 * B_V), rows),
        ],
        out_specs=pl.BlockSpec((1, s, 2 * B_V), rows),
        out_shape=jax.ShapeDtypeStruct((b, s, B_HEADS * B_V), BF16),
        scratch_shapes=[pltpu.VMEM((2, 2, span * tk, qb), F32)],
        compiler_params=_params("parallel", "arbitrary"),
        name="b_attn",
    )(q, k, vt, gate)


def _out_proj_kernel(x_ref, y_ref, w_ref, fg_ref, o_ref, *, final_norm):
    xn = x_ref[...] + jnp.dot(y_ref[...], w_ref[...], preferred_element_type=F32)
    if final_norm:
        xn = _rms(xn, fg_ref[...])
    o_ref[...] = xn


def _out_proj(x2, y2, w_bf, fg, final_norm, tm=1024):
    t = x2.shape[0]
    tok = pl.BlockSpec((tm, D_MODEL), lambda i: (i, 0))
    return pl.pallas_call(
        functools.partial(_out_proj_kernel, final_norm=final_norm),
        grid=(t // tm,),
        in_specs=[tok, tok,
                  pl.BlockSpec((D_MODEL, D_MODEL), lambda i: (0, 0)),
                  pl.BlockSpec((1, D_MODEL), lambda i: (0, 0))],
        out_specs=tok,
        out_shape=jax.ShapeDtypeStruct((t, D_MODEL), F32),
        compiler_params=_params("parallel"),
        name="out_proj",
    )(x2, y2, w_bf, fg)


def _mla_layer(x, o_prev, gate_prev, w_out_prev, g, wts, coeffs, fg, final_norm):
    b, s, _ = x.shape
    w1, qg, wq, kvg, wk, wvt, w_out = wts
    x, q, k, vt, gate = _b_proj(x, o_prev, gate_prev, w_out_prev, g, w1, qg, wq, kvg, wk, wvt,
                                coeffs)
    y = _mla_attn(q, k, vt, gate)
    out = _out_proj(x.reshape(b * s, D_MODEL), y.reshape(b * s, D_MODEL), w_out, fg, final_norm)
    return out.reshape(b, s, D_MODEL)


def _prep_mla_weights(w_in, q_g, w_q_up, kv_g, w_kv_up, w_out):
    c0, c1, c2 = B_Q_LORA, B_Q_LORA + B_KV_LORA, B_Q_LORA + B_KV_LORA + B_ROPE
    pad_r = B_QK_PAD - B_NOPE - B_ROPE
    w_kr = jnp.pad(w_in[:, c1:c2], ((0, 0), (B_NOPE, pad_r)))
    w1 = jnp.concatenate([w_in[:, :c1], w_kr, w_in[:, c2:]], axis=1).astype(BF16)
    wq = w_q_up.reshape(B_Q_LORA, B_HEADS, B_NOPE + B_ROPE)
    wq = jnp.pad(wq, ((0, 0), (0, 0), (0, pad_r))).reshape(B_Q_LORA, B_HEADS * B_QK_PAD)
    wkv = w_kv_up.reshape(B_KV_LORA, B_HEADS, B_NOPE + B_V)
    wk = jnp.pad(wkv[:, :, :B_NOPE], ((0, 0), (0, 0), (0, B_QK_PAD - B_NOPE)))
    wk = wk.reshape(B_KV_LORA, B_HEADS * B_QK_PAD)
    wvt = wkv[:, :, B_NOPE:].reshape(B_KV_LORA, B_HEADS * B_V).T
    return (w1, q_g.reshape(1, -1), wq.astype(BF16), kv_g.reshape(1, -1),
            wk.astype(BF16), wvt.astype(BF16), w_out.astype(BF16))


def _trunk(x, norm_g, a_wts, b_wts, coeffs, fg):
    for i in range(DEPTH):
        g = norm_g[i].reshape(1, D_MODEL)
        j = i // 2
        if i % 2 == 0:
            w_in_a, w_out_a = a_wts[j]
            o_a, gate_a = _dilated_layer(x, g, w_in_a)
        else:
            x = _mla_layer(x, o_a, gate_a, w_out_a, g, b_wts[j], coeffs, fg,
                           final_norm=(i == DEPTH - 1))
    return x


def kernel(x_prompt, x_sample, norm_g, a_w_in, a_w_out, b_w_in, b_q_norm_g, b_w_q_up,
           b_kv_norm_g, b_w_kv_up, b_w_out, final_norm_g):
    a_wts = [(a_w_in[j].astype(BF16).reshape(D_MODEL, A_NQKV + 1, A_GROUP_WIDTH).transpose(1, 0, 2),
              a_w_out[j].astype(BF16)) for j in range(a_w_in.shape[0])]
    b_wts = [_prep_mla_weights(b_w_in[j], b_q_norm_g[j], b_w_q_up[j], b_kv_norm_g[j],
                               b_w_kv_up[j], b_w_out[j]) for j in range(b_w_in.shape[0])]
    fg = final_norm_g.reshape(1, D_MODEL)
    outs = []
    for x in (x_prompt, x_sample):
        coeffs = _rope_coeffs(x.shape[1])
        outs.append(_trunk(x, norm_g, a_wts, b_wts, coeffs, fg))
    return tuple(outs)
```

```python
import functools

import numpy as np
import jax
import jax.numpy as jnp
from jax import lax
from jax.experimental import pallas as pl
from jax.experimental.pallas import tpu as pltpu

F32 = jnp.float32
BF16 = jnp.bfloat16

D_MODEL = 1024
DEPTH = 4
RMS_EPS = 1e-6
NEG_BIG = -1e30
LOG2E = float(np.log2(np.e))
LANES = 128

A_GROUPS = ((128, 1), (512, 4), (2048, 16))
A_DILS = tuple(d for _, d in A_GROUPS)
A_HEADS = 8
A_HEAD_DIM = 128
A_GROUP_WIDTH = A_HEADS * A_HEAD_DIM
A_NQKV = 3 * len(A_GROUPS)
A_PREP_STEP = A_NQKV - 2
ALIBI_MAX_EXP = 8.0
A_RADIUS = 64
A_QSUB = 128
A_KWIN = A_QSUB + 2 * A_RADIUS
A_TILES_PER_ITER = 8
A_SCORE_AHEAD = 2
A_MERGE_BEHIND = 2
A_Q_PRESCALE = float(LOG2E * A_HEAD_DIM ** -0.5)

B_HEADS = 16
B_NOPE = 64
B_ROPE = 32
B_V = 64
B_Q_LORA = 384
B_KV_LORA = 256
B_QK_PAD = 128
B_ONES_ROWS = 16
ROPE_THETA = 10000.0
B_Q_PRESCALE = float(LOG2E * (B_NOPE + B_ROPE) ** -0.5)

VMEM_LIMIT = 56 * 1024 * 1024


def _params(*sem):
    return pltpu.CompilerParams(dimension_semantics=sem, vmem_limit_bytes=VMEM_LIMIT)


def _rms(xf, g):
    ms = jnp.mean(xf * xf, axis=-1, keepdims=True)
    return xf * lax.rsqrt(ms + RMS_EPS) * g


def _silu(g):
    return g * (1.0 / (1.0 + jnp.exp(-g)))


def _dot_nt(a, b):
    return lax.dot_general(a, b, (((1,), (1,)), ((), ())), preferred_element_type=F32)


def _a_proj_kernel(x_ref, g_ref, w_ref, qkv0_ref, qkv1_ref, qkv2_ref, gate_ref,
                   h_scr, h1_scr, h2_scr, hf_scr, hfm_scr):
    blk = pl.program_id(0) * pl.num_programs(1) + pl.program_id(1)
    j = pl.program_id(2)
    slot = blk % 2
    tm = x_ref.shape[1]
    nslab = D_MODEL // LANES

    def normalise(dst_slot, part=0, nparts=1):
        n = tm // nparts
        rows = slice(part * n, (part + 1) * n)
        h = _rms(x_ref[0, rows, :], g_ref[...])
        h_scr[dst_slot, rows, :] = h.astype(BF16)
        for c in range(nslab):
            hf_scr[c * tm + part * n:c * tm + (part + 1) * n, :] = h[:, c * LANES:(c + 1) * LANES]

    ratio = A_DILS[2] // A_DILS[1]
    assert A_DILS[1] == ratio

    def permute_mid(residues=range(A_DILS[1])):
        n = tm // A_DILS[1]
        for r in residues:
            for c in range(nslab):
                piece = hf_scr[pl.ds(c * tm + r, n, stride=A_DILS[1]), :]
                hfm_scr[c * tm + r * n:c * tm + (r + 1) * n, :] = piece
                h1_scr[r * n:(r + 1) * n, c * LANES:(c + 1) * LANES] = piece.astype(BF16)

    def permute_wide(residues=range(A_DILS[2])):
        n_mid = tm // A_DILS[1]
        n = tm // A_DILS[2]
        for r in residues:
            r_mid, a = r % A_DILS[1], r // A_DILS[1]
            for c in range(nslab):
                piece = hfm_scr[pl.ds(c * tm + r_mid * n_mid + a, n, stride=ratio), :]
                h2_scr[r * n:(r + 1) * n, c * LANES:(c + 1) * LANES] = piece.astype(BF16)

    def project(lhs):
        acc = jnp.dot(lhs, w_ref[0], preferred_element_type=F32)
        return acc * jnp.where(j % 3 == 0, A_Q_PRESCALE, 1.0)

    nchunk = 4
    cw = A_GROUP_WIDTH // nchunk

    def project_group2(chunk=None):
        cols = slice(None) if chunk is None else slice(chunk * cw, (chunk + 1) * cw)
        acc = jnp.dot(h2_scr[...], w_ref[0, :, cols], preferred_element_type=F32)
        acc = (acc * jnp.where(j % 3 == 0, A_Q_PRESCALE, 1.0)).astype(BF16)
        n = tm // A_DILS[2]
        for r in range(A_DILS[2]):
            qkv2_ref[0, 0, r, :, cols] = acc[r * n:(r + 1) * n]

    @pl.when((blk == 0) & (j == 0))
    def _():
        normalise(0)
        permute_mid()
        permute_wide()

    @pl.when(j < 3)
    def _():
        qkv0_ref[0, 0, 0] = project(h_scr[slot]).astype(BF16)

    @pl.when((j >= 3) & (j < 6))
    def _():
        acc = project(h1_scr[...]).astype(BF16)
        n = tm // A_DILS[1]
        for r in range(A_DILS[1]):
            qkv1_ref[0, 0, r] = acc[r * n:(r + 1) * n]

    @pl.when(j == 6)
    def _():
        project_group2()

    @pl.when(j == A_PREP_STEP)
    def _():
        for q in range(nchunk):
            project_group2(q)
            normalise(1 - slot, q, nchunk)

    @pl.when(j == A_PREP_STEP + 1)
    def _():
        per = A_DILS[1] // nchunk
        for q in range(nchunk):
            project_group2(q)
            permute_mid(range(q * per, (q + 1) * per))

    @pl.when(j == A_NQKV)
    def _():
        per = A_DILS[2] // nchunk
        for q in range(nchunk):
            gate_ref[0, :, q * cw:(q + 1) * cw] = jnp.dot(
                h_scr[slot], w_ref[0, :, q * cw:(q + 1) * cw], preferred_element_type=F32)
            permute_wide(range(q * per, (q + 1) * per))


def _a_proj1_kernel(x_ref, g_ref, w_ref, qkv0_ref, qkv1_ref, qkv2_ref, gate_ref,
                    h_scr, h1_scr, h2_scr, hf_scr, hfm_scr):
    tm = x_ref.shape[1]
    nslab = D_MODEL // LANES
    ratio = A_DILS[2] // A_DILS[1]
    assert A_DILS[1] == ratio
    n_mid = tm // A_DILS[1]
    n_wide = tm // A_DILS[2]

    h = _rms(x_ref[0], g_ref[...])
    h_scr[...] = h.astype(BF16)
    for c in range(nslab):
        hf_scr[c * tm:(c + 1) * tm, :] = h[:, c * LANES:(c + 1) * LANES]

    def permute_mid(r):
        for c in range(nslab):
            piece = hf_scr[pl.ds(c * tm + r, n_mid, stride=A_DILS[1]), :]
            hfm_scr[c * tm + r * n_mid:c * tm + (r + 1) * n_mid, :] = piece
            h1_scr[r * n_mid:(r + 1) * n_mid, c * LANES:(c + 1) * LANES] = piece.astype(BF16)

    def permute_wide(r):
        r_mid, a = r % A_DILS[1], r // A_DILS[1]
        for c in range(nslab):
            piece = hfm_scr[pl.ds(c * tm + r_mid * n_mid + a, n_wide, stride=ratio), :]
            h2_scr[r * n_wide:(r + 1) * n_wide, c * LANES:(c + 1) * LANES] = piece.astype(BF16)

    def project(lhs_scr, m):
        acc = jnp.dot(lhs_scr[...], w_ref[m], preferred_element_type=F32)
        return acc * A_Q_PRESCALE if m % 3 == 0 else acc

    for m in range(3):
        qkv0_ref[0, m, 0] = project(h_scr, m).astype(BF16)
        if m < 2:
            for r in range(m * 2, m * 2 + 2):
                permute_mid(r)
    gate_ref[0] = jnp.dot(h_scr[...], w_ref[A_NQKV], preferred_element_type=F32)
    for m in range(3, 6):
        acc = project(h1_scr, m).astype(BF16)
        for r in range(A_DILS[1]):
            qkv1_ref[0, m - 3, r] = acc[r * n_mid:(r + 1) * n_mid]
        if m == 3:
            for r in range(A_DILS[2]):
                permute_wide(r)
    for m in range(6, 9):
        acc = project(h2_scr, m).astype(BF16)
        for r in range(A_DILS[2]):
            qkv2_ref[0, m - 6, r] = acc[r * n_wide:(r + 1) * n_wide]


def _a_proj1(x, g, w_bf, tm=512):
    b, s, _ = x.shape
    gw = A_GROUP_WIDTH
    tok = lambda bb, ii: (bb, ii, 0)
    return pl.pallas_call(
        _a_proj1_kernel,
        grid=(b, s // tm),
        in_specs=[
            pl.BlockSpec((1, tm, D_MODEL), tok),
            pl.BlockSpec((1, D_MODEL), lambda bb, ii: (0, 0)),
            pl.BlockSpec((A_NQKV + 1, D_MODEL, gw), lambda bb, ii: (0, 0, 0),
                         pipeline_mode=pl.Buffered(1)),
        ],
        out_specs=[pl.BlockSpec((1, 3, d, tm // d, gw), lambda bb, ii: (bb, 0, 0, ii, 0))
                   for d in A_DILS] + [pl.BlockSpec((1, tm, gw), tok)],
        out_shape=[jax.ShapeDtypeStruct((b, 3, d, s // d, gw), BF16) for d in A_DILS]
        + [jax.ShapeDtypeStruct((b, s, gw), F32)],
        scratch_shapes=[pltpu.VMEM((tm, D_MODEL), BF16),
                        pltpu.VMEM((tm, D_MODEL), BF16),
                        pltpu.VMEM((tm, D_MODEL), BF16),
                        pltpu.VMEM((D_MODEL // LANES * tm, LANES), F32),
                        pltpu.VMEM((D_MODEL // LANES * tm, LANES), F32)],
        compiler_params=_params("parallel", "parallel"),
        name="a_proj",
    )(x, g, w_bf)


def _a_proj(x, g, w_bf, tm=1024):
    b, s, _ = x.shape
    gw = A_GROUP_WIDTH

    def qkv_spec(group):
        dil = A_DILS[group]
        return pl.BlockSpec((1, 1, dil, tm // dil, gw),
                            lambda bb, ii, jj: (bb, jnp.clip(jj - 3 * group, 0, 2), 0, ii, 0))

    nblk = s // tm

    def x_index(bb, ii, jj):
        nxt = jnp.minimum(bb * nblk + ii + (jj >= A_PREP_STEP).astype(jnp.int32), b * nblk - 1)
        return (nxt // nblk, nxt % nblk, 0)

    return pl.pallas_call(
        _a_proj_kernel,
        grid=(b, nblk, A_NQKV + 1),
        in_specs=[
            pl.BlockSpec((1, tm, D_MODEL), x_index),
            pl.BlockSpec((1, D_MODEL), lambda bb, ii, jj: (0, 0)),
            pl.BlockSpec((1, D_MODEL, gw), lambda bb, ii, jj: (jj, 0, 0)),
        ],
        out_specs=[qkv_spec(0), qkv_spec(1), qkv_spec(2),
                   pl.BlockSpec((1, tm, gw), lambda bb, ii, jj: (bb, ii, 0))],
        out_shape=[jax.ShapeDtypeStruct((b, 3, d, s // d, gw), BF16) for d in A_DILS]
        + [jax.ShapeDtypeStruct((b, s, gw), F32)],
        scratch_shapes=[pltpu.VMEM((2, tm, D_MODEL), BF16),
                        pltpu.VMEM((tm, D_MODEL), BF16),
                        pltpu.VMEM((tm, D_MODEL), BF16),
                        pltpu.VMEM((D_MODEL // LANES * tm, LANES), F32),
                        pltpu.VMEM((D_MODEL // LANES * tm, LANES), F32)],
        compiler_params=_params("arbitrary", "arbitrary", "arbitrary"),
        name="a_proj",
    )(x, g, w_bf)


def _a_attn_kernel(q0, k0, v0, q1, k1, v1, q2, k2, v2, bias_ref, out_ref, m_scr, l_scr):
    o_ref = out_ref.at[0]
    seq = o_ref.shape[0]
    ntile = seq // A_QSUB
    ones_cols = jnp.ones((A_KWIN, LANES), BF16)

    def place(dil, t):
        seq_l = seq // dil
        nsub = seq_l // A_QSUB
        r = t // nsub
        sub = t % nsub
        row0 = pl.multiple_of(sub * A_QSUB, A_QSUB)
        start = pl.multiple_of(jnp.clip(row0 - A_RADIUS, 0, seq_l - A_KWIN), A_RADIUS)
        case = jnp.where(sub == 0, 0, jnp.where(sub == nsub - 1, 2, 1))
        if dil == 1:
            rows = pl.ds(row0, A_QSUB)
        else:
            rows = pl.ds(row0 * dil + r, A_QSUB, stride=dil)
        return r, row0, start, case, rows

    first = len(A_GROUPS) - 1
    for group, refs in reversed(tuple(enumerate(((q0, k0, v0), (q1, k1, v1), (q2, k2, v2))))):
        dil = A_DILS[group]

        def step(it, carry, group=group, refs=refs, dil=dil):
            q_ref, k_ref, v_ref = refs
            n = A_TILES_PER_ITER
            places = [place(dil, it * n + u) for u in range(n)]
            scs, stats = {}, {}

            def score(u):
                r, row0, start, case, _ = places[u]
                scs[u] = (_dot_nt(q_ref[0, 0, r, pl.ds(row0, A_QSUB), :],
                                  k_ref[0, 0, r, pl.ds(start, A_KWIN), :])
                          + bias_ref[0, 3 * group + case])

            def softmax_values(u):
                r, _, start, _, _ = places[u]
                sc = scs.pop(u)
                m = jnp.max(sc, axis=-1, keepdims=True)
                p = jnp.exp2(sc - m).astype(BF16)
                v1 = jnp.concatenate([v_ref[0, 0, r, pl.ds(start, A_KWIN), :], ones_cols], axis=1)
                acc = jnp.dot(p, v1, preferred_element_type=F32)
                stats[u] = (m, acc[:, A_HEAD_DIM:], acc[:, :A_HEAD_DIM])

            def merge(u):
                rows = places[u][4]
                m, l, acc = stats.pop(u)
                if group == first:
                    m_scr[rows, :] = jnp.broadcast_to(m, (A_QSUB, LANES))
                    l_scr[rows, :] = l
                    o_ref[rows, :] = acc
                else:
                    m_old = m_scr[rows, :]
                    m_new = jnp.maximum(m_old, m)
                    a_old = jnp.exp2(m_old - m_new)
                    a_cur = jnp.exp2(m - m_new)
                    m_scr[rows, :] = m_new
                    l_scr[rows, :] = l_scr[rows, :] * a_old + l * a_cur
                    o_ref[rows, :] = o_ref[rows, :] * a_old + acc * a_cur

            for s in range(n + A_SCORE_AHEAD + A_MERGE_BEHIND):
                if s < n:
                    score(s)
                if 0 <= s - A_SCORE_AHEAD < n:
                    softmax_values(s - A_SCORE_AHEAD)
                if 0 <= s - A_SCORE_AHEAD - A_MERGE_BEHIND < n:
                    merge(s - A_SCORE_AHEAD - A_MERGE_BEHIND)
            return carry

        lax.fori_loop(0, ntile // A_TILES_PER_ITER, step, 0, unroll=True)

    o_ref[...] = o_ref[...] / l_scr[...]


def _alibi_bias_tables():
    n = len(A_GROUPS) * A_HEADS
    slopes = (2.0 ** (-ALIBI_MAX_EXP * np.arange(1, n + 1) / n)).astype(np.float32)
    slopes = slopes.reshape(len(A_GROUPS), A_HEADS)
    i = np.arange(A_QSUB)[:, None]
    j = np.arange(A_KWIN)[None, :]
    out = np.empty((A_HEADS, 3 * len(A_GROUPS), A_QSUB, A_KWIN), np.float32)
    for g, dil in enumerate(A_DILS):
        for case, shift in enumerate((0, -A_RADIUS, -2 * A_RADIUS)):
            delta = np.abs(j - i + shift)
            bias = (-slopes[g][:, None, None] * (delta * dil)[None]).astype(np.float32)
            bias = bias * np.float32(LOG2E)
            out[:, 3 * g + case] = np.where((delta <= A_RADIUS)[None], bias, np.float32(NEG_BIG))
    return out


def _a_attn(qkvs, seq):
    b = qkvs[0].shape[0]
    in_specs, args = [], []
    for qkv, dil in zip(qkvs, A_DILS):
        for n in range(3):
            in_specs.append(pl.BlockSpec((1, 1, dil, seq // dil, A_HEAD_DIM),
                                         lambda bb, hh, n=n: (bb, n, 0, 0, hh)))
            args.append(qkv)
    in_specs.append(pl.BlockSpec((1, 3 * len(A_GROUPS), A_QSUB, A_KWIN),
                                 lambda bb, hh: (hh, 0, 0, 0)))
    args.append(jnp.asarray(_alibi_bias_tables()))
    return pl.pallas_call(
        _a_attn_kernel,
        grid=(b, A_HEADS),
        in_specs=in_specs,
        out_specs=pl.BlockSpec((1, seq, A_HEAD_DIM), lambda bb, hh: (bb, 0, hh)),
        out_shape=jax.ShapeDtypeStruct((b, seq, A_GROUP_WIDTH), F32),
        scratch_shapes=[pltpu.VMEM((seq, LANES), F32), pltpu.VMEM((seq, LANES), F32)],
        compiler_params=_params("parallel", "arbitrary"),
        name="a_attn",
    )(*args)


def _dilated_layer(x, g, w_in_bf):
    qkv0, qkv1, qkv2, gate = _a_proj1(x, g, w_in_bf)
    return _a_attn((qkv0, qkv1, qkv2), x.shape[1]), gate


def _rope(t, ca, cm, cp):
    return t * ca + pltpu.roll(t, 128 - B_ROPE // 2, 1) * cm + pltpu.roll(t, B_ROPE // 2, 1) * cp


def _b_proj_kernel(x_ref, o_ref, pgate_ref, wprev_ref, g_ref, w1_ref, qg_ref, wq_ref, kvg_ref,
                   wk_ref, wvt_ref, ca_ref, cm_ref, cp_ref,
                   xnew_ref, q_ref, k_ref, vt_ref, gate_ref):
    y = (o_ref[0] * _silu(pgate_ref[0])).astype(BF16)
    x = x_ref[0] + jnp.dot(y, wprev_ref[...], preferred_element_type=F32)
    xnew_ref[0] = x
    h = _rms(x, g_ref[...]).astype(BF16)
    proj = jnp.dot(h, w1_ref[...], preferred_element_type=F32)
    c_q = proj[:, :B_Q_LORA]
    c_kv = proj[:, B_Q_LORA:B_Q_LORA + B_KV_LORA]
    k_r = proj[:, B_Q_LORA + B_KV_LORA:B_Q_LORA + B_KV_LORA + B_QK_PAD]
    gate_ref[0] = proj[:, B_Q_LORA + B_KV_LORA + B_QK_PAD:]
    ca, cm, cp = ca_ref[...], cm_ref[...], cp_ref[...]

    qn = _rms(c_q, qg_ref[...]).astype(BF16)
    q = jnp.dot(qn, wq_ref[...], preferred_element_type=F32)
    for hh in range(B_HEADS):
        cs = slice(hh * B_QK_PAD, (hh + 1) * B_QK_PAD)
        q_ref[0, :, cs] = (_rope(q[:, cs], ca, cm, cp) * B_Q_PRESCALE).astype(BF16)

    kvn = _rms(c_kv, kvg_ref[...]).astype(BF16)
    kn = jnp.dot(kvn, wk_ref[...], preferred_element_type=F32)
    k_rot = _rope(k_r, ca, cm, cp)
    for hh in range(B_HEADS):
        cs = slice(hh * B_QK_PAD, (hh + 1) * B_QK_PAD)
        k_ref[0, :, cs] = (kn[:, cs] + k_rot).astype(BF16)
    vt_ref[0] = _dot_nt(wvt_ref[...], kvn).astype(BF16)


def _rope_coeffs(s):
    half = B_ROPE // 2
    inv_freq = (ROPE_THETA ** (-np.arange(0, B_ROPE, 2) / B_ROPE)).astype(np.float32)
    ang = jnp.arange(s, dtype=F32)[:, None] * jnp.asarray(inv_freq)[None, :]
    cos, sin = jnp.cos(ang), jnp.sin(ang)
    ones = jnp.ones((s, B_NOPE), F32)
    zn = jnp.zeros((s, B_NOPE), F32)
    zh = jnp.zeros((s, half), F32)
    zt = jnp.zeros((s, B_QK_PAD - B_NOPE - B_ROPE), F32)
    ca = jnp.concatenate([ones, cos, cos, zt], axis=1)
    cm = jnp.concatenate([zn, -sin, zh, zt], axis=1)
    cp = jnp.concatenate([zn, zh, sin, zt], axis=1)
    return ca, cm, cp


def _b_proj(x, o_prev, gate_prev, w_out_prev, g, w1, qg, wq, kvg, wk, wvt, coeffs, tm=512):
    b, s, _ = x.shape
    n1 = w1.shape[1]
    const = lambda bb, ii: (0, 0)
    tok = lambda bb, ii: (bb, ii, 0)
    qk_w = B_HEADS * B_QK_PAD
    v_w = B_HEADS * B_V
    return pl.pallas_call(
        _b_proj_kernel,
        grid=(b, s // tm),
        in_specs=[
            pl.BlockSpec((1, tm, D_MODEL), tok),
            pl.BlockSpec((1, tm, A_GROUP_WIDTH), tok),
            pl.BlockSpec((1, tm, A_GROUP_WIDTH), tok),
            pl.BlockSpec((A_GROUP_WIDTH, D_MODEL), const),
            pl.BlockSpec((1, D_MODEL), const),
            pl.BlockSpec((D_MODEL, n1), const),
            pl.BlockSpec((1, B_Q_LORA), const),
            pl.BlockSpec((B_Q_LORA, qk_w), const),
            pl.BlockSpec((1, B_KV_LORA), const),
            pl.BlockSpec((B_KV_LORA, qk_w), const),
            pl.BlockSpec((v_w, B_KV_LORA), const),
            pl.BlockSpec((tm, B_QK_PAD), lambda bb, ii: (ii, 0)),
            pl.BlockSpec((tm, B_QK_PAD), lambda bb, ii: (ii, 0)),
            pl.BlockSpec((tm, B_QK_PAD), lambda bb, ii: (ii, 0)),
        ],
        out_specs=[
            pl.BlockSpec((1, tm, D_MODEL), tok),
            pl.BlockSpec((1, tm, qk_w), tok),
            pl.BlockSpec((1, tm, qk_w), tok),
            pl.BlockSpec((1, v_w, tm), lambda bb, ii: (bb, 0, ii)),
            pl.BlockSpec((1, tm, v_w), tok),
        ],
        out_shape=[
            jax.ShapeDtypeStruct((b, s, D_MODEL), F32),
            jax.ShapeDtypeStruct((b, s, qk_w), BF16),
            jax.ShapeDtypeStruct((b, s, qk_w), BF16),
            jax.ShapeDtypeStruct((b, v_w, s), BF16),
            jax.ShapeDtypeStruct((b, s, v_w), F32),
        ],
        compiler_params=_params("parallel", "parallel"),
        name="b_proj",
    )(x, o_prev, gate_prev, w_out_prev, g, w1, qg, wq, kvg, wk, wvt, *coeffs)


def _mla_attn_kernel(q_ref, k_ref, vt_ref, gate_ref, y_ref, st_scr, *, qb, tk, span):
    s = k_ref.shape[1]
    nq = s // qb
    nspan = s // (span * tk)

    def keys_of(c):
        return pl.ds(pl.multiple_of(c * tk, tk), tk)

    def queries_of(qi):
        return pl.ds(pl.multiple_of(qi * qb, qb), qb)

    def scores(qi, sp, slot):
        keys = pl.ds(pl.multiple_of(sp * (span * tk), span * tk), span * tk)
        for hd in range(2):
            cs = slice(hd * B_QK_PAD, (hd + 1) * B_QK_PAD)
            st_scr[slot, hd] = _dot_nt(k_ref[0, keys, cs], q_ref[0, queries_of(qi), cs])

    ones_rows = jnp.ones((B_ONES_ROWS, tk), BF16)

    def update(c, slot, part, carry):
        new = []
        for hd in range(2):
            m, acc = carry[hd]
            vrows = slice(hd * B_V, (hd + 1) * B_V)
            st = st_scr[slot, hd, part * tk:(part + 1) * tk, :]
            mn = jnp.maximum(m, jnp.max(st, axis=0, keepdims=True))
            alpha = jnp.exp2(m - mn)
            p = jnp.exp2(st - mn).astype(BF16)
            v1 = jnp.concatenate([vt_ref[0, vrows, keys_of(c)], ones_rows], axis=0)
            acc = acc * alpha + jnp.dot(v1, p, preferred_element_type=F32)
            new.append((mn, acc))
        return tuple(new)

    def qblock(qi, _):
        init = (jnp.full((1, qb), NEG_BIG, F32), jnp.zeros((B_V + B_ONES_ROWS, qb), F32))
        carry = (init, init)
        for sp in range(nspan):
            if sp + 1 < nspan:
                scores(qi, sp + 1, (sp + 1) % 2)
            else:
                scores(jnp.minimum(qi + 1, nq - 1), 0, (sp + 1) % 2)
            for part in range(span):
                carry = update(sp * span + part, sp % 2, part, carry)
        outs = [acc[:B_V] / acc[B_V:B_V + 1] for _, acc in carry]
        o = jnp.concatenate(outs, axis=0).T
        rows = queries_of(qi)
        y_ref[0, rows, :] = (o * _silu(gate_ref[0, rows, :])).astype(BF16)
        return 0

    scores(0, 0, 0)
    lax.fori_loop(0, nq, qblock, 0, unroll=4)


def _mla_attn(q, k, vt, gate, qb=256, tk=256, span=2):
    b, s, _ = q.shape
    assert (s // (span * tk)) % 2 == 0
    rows = lambda bb, hp: (bb, 0, hp)
    return pl.pallas_call(
        functools.partial(_mla_attn_kernel, qb=qb, tk=tk, span=span),
        grid=(b, B_HEADS // 2),
        in_specs=[
            pl.BlockSpec((1, s, 2 * B_QK_PAD), rows),
            pl.BlockSpec((1, s, 2 * B_QK_PAD), rows),
            pl.BlockSpec((1, 2 * B_V, s), lambda bb, hp: (bb, hp, 0)),
            pl.BlockSpec((1, s, 2 * B_V), rows),
        ],
        out_specs=pl.BlockSpec((1, s, 2 * B_V), rows),
        out_shape=jax.ShapeDtypeStruct((b, s, B_HEADS * B_V), BF16),
        scratch_shapes=[pltpu.VMEM((2, 2, span * tk, qb), F32)],
        compiler_params=_params("parallel", "arbitrary"),
        name="b_attn",
    )(q, k, vt, gate)


def _out_proj_kernel(x_ref, y_ref, w_ref, fg_ref, o_ref, *, final_norm):
    xn = x_ref[...] + jnp.dot(y_ref[...], w_ref[...], preferred_element_type=F32)
    if final_norm:
        xn = _rms(xn, fg_ref[...])
    o_ref[...] = xn


def _out_proj(x2, y2, w_bf, fg, final_norm, tm=1024):
    t = x2.shape[0]
    tok = pl.BlockSpec((tm, D_MODEL), lambda i: (i, 0))
    return pl.pallas_call(
        functools.partial(_out_proj_kernel, final_norm=final_norm),
        grid=(t // tm,),
        in_specs=[tok, tok,
                  pl.BlockSpec((D_MODEL, D_MODEL), lambda i: (0, 0)),
                  pl.BlockSpec((1, D_MODEL), lambda i: (0, 0))],
        out_specs=tok,
        out_shape=jax.ShapeDtypeStruct((t, D_MODEL), F32),
        compiler_params=_params("parallel"),
        name="out_proj",
    )(x2, y2, w_bf, fg)


def _mla_layer(x, o_prev, gate_prev, w_out_prev, g, wts, coeffs, fg, final_norm):
    b, s, _ = x.shape
    w1, qg, wq, kvg, wk, wvt, w_out = wts
    x, q, k, vt, gate = _b_proj(x, o_prev, gate_prev, w_out_prev, g, w1, qg, wq, kvg, wk, wvt,
                                coeffs)
    y = _mla_attn(q, k, vt, gate)
    out = _out_proj(x.reshape(b * s, D_MODEL), y.reshape(b * s, D_MODEL), w_out, fg, final_norm)
    return out.reshape(b, s, D_MODEL)


def _prep_mla_weights(w_in, q_g, w_q_up, kv_g, w_kv_up, w_out):
    c0, c1, c2 = B_Q_LORA, B_Q_LORA + B_KV_LORA, B_Q_LORA + B_KV_LORA + B_ROPE
    pad_r = B_QK_PAD - B_NOPE - B_ROPE
    w_kr = jnp.pad(w_in[:, c1:c2], ((0, 0), (B_NOPE, pad_r)))
    w1 = jnp.concatenate([w_in[:, :c1], w_kr, w_in[:, c2:]], axis=1).astype(BF16)
    wq = w_q_up.reshape(B_Q_LORA, B_HEADS, B_NOPE + B_ROPE)
    wq = jnp.pad(wq, ((0, 0), (0, 0), (0, pad_r))).reshape(B_Q_LORA, B_HEADS * B_QK_PAD)
    wkv = w_kv_up.reshape(B_KV_LORA, B_HEADS, B_NOPE + B_V)
    wk = jnp.pad(wkv[:, :, :B_NOPE], ((0, 0), (0, 0), (0, B_QK_PAD - B_NOPE)))
    wk = wk.reshape(B_KV_LORA, B_HEADS * B_QK_PAD)
    wvt = wkv[:, :, B_NOPE:].reshape(B_KV_LORA, B_HEADS * B_V).T
    return (w1, q_g.reshape(1, -1), wq.astype(BF16), kv_g.reshape(1, -1),
            wk.astype(BF16), wvt.astype(BF16), w_out.astype(BF16))


def _trunk(x, norm_g, a_wts, b_wts, coeffs, fg):
    for i in range(DEPTH):
        g = norm_g[i].reshape(1, D_MODEL)
        j = i // 2
        if i % 2 == 0:
            w_in_a, w_out_a = a_wts[j]
            o_a, gate_a = _dilated_layer(x, g, w_in_a)
        else:
            x = _mla_layer(x, o_a, gate_a, w_out_a, g, b_wts[j], coeffs, fg,
                           final_norm=(i == DEPTH - 1))
    return x


def kernel(x_prompt, x_sample, norm_g, a_w_in, a_w_out, b_w_in, b_q_norm_g, b_w_q_up,
           b_kv_norm_g, b_w_kv_up, b_w_out, final_norm_g):
    a_wts = [(a_w_in[j].astype(BF16).reshape(D_MODEL, A_NQKV + 1, A_GROUP_WIDTH).transpose(1, 0, 2),
              a_w_out[j].astype(BF16)) for j in range(a_w_in.shape[0])]
    b_wts = [_prep_mla_weights(b_w_in[j], b_q_norm_g[j], b_w_q_up[j], b_kv_norm_g[j],
                               b_w_kv_up[j], b_w_out[j]) for j in range(b_w_in.shape[0])]
    fg = final_norm_g.reshape(1, D_MODEL)
    outs = []
    for x in (x_prompt, x_sample):
        coeffs = _rope_coeffs(x.shape[1])
        outs.append(_trunk(x, norm_g, a_wts, b_wts, coeffs, fg))
    return tuple(outs)
```

```python
import functools

import numpy as np
import jax
import jax.numpy as jnp
from jax import lax
from jax.experimental import pallas as pl
from jax.experimental.pallas import tpu as pltpu

F32 = jnp.float32
BF16 = jnp.bfloat16

D_MODEL = 1024
DEPTH = 4
RMS_EPS = 1e-6
NEG_BIG = -1e30
LOG2E = float(np.log2(np.e))
LANES = 128

A_GROUPS = ((128, 1), (512, 4), (2048, 16))
A_DILS = tuple(d for _, d in A_GROUPS)
A_HEADS = 8
A_HEAD_DIM = 128
A_GROUP_WIDTH = A_HEADS * A_HEAD_DIM
A_NQKV = 3 * len(A_GROUPS)
ALIBI_MAX_EXP = 8.0
A_RADIUS = 64
A_QSUB = 128
A_KWIN = A_QSUB + 2 * A_RADIUS
A_TILES_PER_ITER = 8
A_SCORE_AHEAD = 2
A_MERGE_BEHIND = 2
A_Q_PRESCALE = float(LOG2E * A_HEAD_DIM ** -0.5)

B_HEADS = 16
B_NOPE = 64
B_ROPE = 32
B_V = 64
B_Q_LORA = 384
B_KV_LORA = 256
B_QK_PAD = 128
B_ONES_ROWS = 16
ROPE_THETA = 10000.0
B_Q_PRESCALE = float(LOG2E * (B_NOPE + B_ROPE) ** -0.5)

VMEM_LIMIT = 56 * 1024 * 1024


def _params(*sem):
    return pltpu.CompilerParams(dimension_semantics=sem, vmem_limit_bytes=VMEM_LIMIT)


def _rms(xf, g):
    ms = jnp.mean(xf * xf, axis=-1, keepdims=True)
    return xf * lax.rsqrt(ms + RMS_EPS) * g


def _silu(g):
    return g * (1.0 / (1.0 + jnp.exp(-g)))


def _dot_nt(a, b):
    return lax.dot_general(a, b, (((1,), (1,)), ((), ())), preferred_element_type=F32)


def _a_proj_kernel(x_ref, g_ref, w_ref, qkv0_ref, qkv1_ref, qkv2_ref, gate_ref,
                   h_scr, h1_scr, h2_scr, hf_scr, hfm_scr):
    tm = x_ref.shape[1]
    nslab = D_MODEL // LANES
    ratio = A_DILS[2] // A_DILS[1]
    assert A_DILS[1] == ratio
    n_mid = tm // A_DILS[1]
    n_wide = tm // A_DILS[2]

    h = _rms(x_ref[0], g_ref[...])
    h_scr[...] = h.astype(BF16)
    for c in range(nslab):
        hf_scr[c * tm:(c + 1) * tm, :] = h[:, c * LANES:(c + 1) * LANES]

    def permute_mid(r):
        for c in range(nslab):
            piece = hf_scr[pl.ds(c * tm + r, n_mid, stride=A_DILS[1]), :]
            hfm_scr[c * tm + r * n_mid:c * tm + (r + 1) * n_mid, :] = piece
            h1_scr[r * n_mid:(r + 1) * n_mid, c * LANES:(c + 1) * LANES] = piece.astype(BF16)

    def permute_wide(r):
        r_mid, a = r % A_DILS[1], r // A_DILS[1]
        for c in range(nslab):
            piece = hfm_scr[pl.ds(c * tm + r_mid * n_mid + a, n_wide, stride=ratio), :]
            h2_scr[r * n_wide:(r + 1) * n_wide, c * LANES:(c + 1) * LANES] = piece.astype(BF16)

    def project(lhs_scr, m):
        acc = jnp.dot(lhs_scr[...], w_ref[m], preferred_element_type=F32)
        return acc * A_Q_PRESCALE if m % 3 == 0 else acc

    def head_cols(hh):
        return slice(hh * A_HEAD_DIM, (hh + 1) * A_HEAD_DIM)

    for m in range(3):
        acc = project(h_scr, m).astype(BF16)
        for hh in range(A_HEADS):
            qkv0_ref[0, m, hh, 0] = acc[:, head_cols(hh)]
        if m < 2:
            for r in range(m * 2, m * 2 + 2):
                permute_mid(r)
    gate_ref[0] = jnp.dot(h_scr[...], w_ref[A_NQKV], preferred_element_type=F32)
    for m in range(3, 6):
        acc = project(h1_scr, m).astype(BF16)
        for r in range(A_DILS[1]):
            for hh in range(A_HEADS):
                qkv1_ref[0, m - 3, hh, r] = acc[r * n_mid:(r + 1) * n_mid, head_cols(hh)]
        if m == 3:
            for r in range(A_DILS[2]):
                permute_wide(r)
    for m in range(6, 9):
        acc = project(h2_scr, m).astype(BF16)
        for r in range(A_DILS[2]):
            for hh in range(A_HEADS):
                qkv2_ref[0, m - 6, hh, r] = acc[r * n_wide:(r + 1) * n_wide, head_cols(hh)]


def _a_proj(x, g, w_bf, tm=512):
    b, s, _ = x.shape
    gw = A_GROUP_WIDTH
    tok = lambda bb, ii: (bb, ii, 0)
    return pl.pallas_call(
        _a_proj_kernel,
        grid=(b, s // tm),
        in_specs=[
            pl.BlockSpec((1, tm, D_MODEL), tok),
            pl.BlockSpec((1, D_MODEL), lambda bb, ii: (0, 0)),
            pl.BlockSpec((A_NQKV + 1, D_MODEL, gw), lambda bb, ii: (0, 0, 0),
                         pipeline_mode=pl.Buffered(1)),
        ],
        out_specs=[pl.BlockSpec((1, 3, A_HEADS, d, tm // d, A_HEAD_DIM),
                                lambda bb, ii: (bb, 0, 0, 0, ii, 0))
                   for d in A_DILS] + [pl.BlockSpec((1, tm, gw), tok)],
        out_shape=[jax.ShapeDtypeStruct((b, 3, A_HEADS, d, s // d, A_HEAD_DIM), BF16)
                   for d in A_DILS] + [jax.ShapeDtypeStruct((b, s, gw), F32)],
        scratch_shapes=[pltpu.VMEM((tm, D_MODEL), BF16),
                        pltpu.VMEM((tm, D_MODEL), BF16),
                        pltpu.VMEM((tm, D_MODEL), BF16),
                        pltpu.VMEM((D_MODEL // LANES * tm, LANES), F32),
                        pltpu.VMEM((D_MODEL // LANES * tm, LANES), F32)],
        compiler_params=_params("parallel", "parallel"),
        name="a_proj",
    )(x, g, w_bf)


def _a_attn_kernel(q0, k0, v0, q1, k1, v1, q2, k2, v2, bias_ref, out_ref, m_scr, l_scr):
    o_ref = out_ref.at[0]
    seq = o_ref.shape[0]
    ntile = seq // A_QSUB
    ones_cols = jnp.ones((A_KWIN, LANES), BF16)

    def place(dil, t):
        seq_l = seq // dil
        nsub = seq_l // A_QSUB
        r = t // nsub
        sub = t % nsub
        row0 = pl.multiple_of(sub * A_QSUB, A_QSUB)
        start = pl.multiple_of(jnp.clip(row0 - A_RADIUS, 0, seq_l - A_KWIN), A_RADIUS)
        case = jnp.where(sub == 0, 0, jnp.where(sub == nsub - 1, 2, 1))
        if dil == 1:
            rows = pl.ds(row0, A_QSUB)
        else:
            rows = pl.ds(row0 * dil + r, A_QSUB, stride=dil)
        return r, row0, start, case, rows

    first = len(A_GROUPS) - 1
    for group, refs in reversed(tuple(enumerate(((q0, k0, v0), (q1, k1, v1), (q2, k2, v2))))):
        dil = A_DILS[group]

        def step(it, carry, group=group, refs=refs, dil=dil):
            q_ref, k_ref, v_ref = refs
            n = A_TILES_PER_ITER
            places = [place(dil, it * n + u) for u in range(n)]
            scs, stats = {}, {}

            def score(u):
                r, row0, start, case, _ = places[u]
                scs[u] = (_dot_nt(q_ref[0, 0, 0, r, pl.ds(row0, A_QSUB), :],
                                  k_ref[0, 0, 0, r, pl.ds(start, A_KWIN), :])
                          + bias_ref[0, 3 * group + case])

            def softmax_values(u):
                r, _, start, _, _ = places[u]
                sc = scs.pop(u)
                m = jnp.max(sc, axis=-1, keepdims=True)
                p = jnp.exp2(sc - m).astype(BF16)
                v1 = jnp.concatenate([v_ref[0, 0, 0, r, pl.ds(start, A_KWIN), :], ones_cols],
                                     axis=1)
                acc = jnp.dot(p, v1, preferred_element_type=F32)
                stats[u] = (m, acc[:, A_HEAD_DIM:], acc[:, :A_HEAD_DIM])

            def merge(u):
                rows = places[u][4]
                m, l, acc = stats.pop(u)
                if group == first:
                    m_scr[rows, :] = jnp.broadcast_to(m, (A_QSUB, LANES))
                    l_scr[rows, :] = l
                    o_ref[rows, :] = acc
                else:
                    m_old = m_scr[rows, :]
                    m_new = jnp.maximum(m_old, m)
                    a_old = jnp.exp2(m_old - m_new)
                    a_cur = jnp.exp2(m - m_new)
                    m_scr[rows, :] = m_new
                    l_scr[rows, :] = l_scr[rows, :] * a_old + l * a_cur
                    o_ref[rows, :] = o_ref[rows, :] * a_old + acc * a_cur

            for s in range(n + A_SCORE_AHEAD + A_MERGE_BEHIND):
                if s < n:
                    score(s)
                if 0 <= s - A_SCORE_AHEAD < n:
                    softmax_values(s - A_SCORE_AHEAD)
                if 0 <= s - A_SCORE_AHEAD - A_MERGE_BEHIND < n:
                    merge(s - A_SCORE_AHEAD - A_MERGE_BEHIND)
            return carry

        lax.fori_loop(0, ntile // A_TILES_PER_ITER, step, 0, unroll=True)

    o_ref[...] = o_ref[...] / l_scr[...]


def _alibi_bias_tables():
    n = len(A_GROUPS) * A_HEADS
    slopes = (2.0 ** (-ALIBI_MAX_EXP * np.arange(1, n + 1) / n)).astype(np.float32)
    slopes = slopes.reshape(len(A_GROUPS), A_HEADS)
    i = np.arange(A_QSUB)[:, None]
    j = np.arange(A_KWIN)[None, :]
    out = np.empty((A_HEADS, 3 * len(A_GROUPS), A_QSUB, A_KWIN), np.float32)
    for g, dil in enumerate(A_DILS):
        for case, shift in enumerate((0, -A_RADIUS, -2 * A_RADIUS)):
            delta = np.abs(j - i + shift)
            bias = (-slopes[g][:, None, None] * (delta * dil)[None]).astype(np.float32)
            bias = bias * np.float32(LOG2E)
            out[:, 3 * g + case] = np.where((delta <= A_RADIUS)[None], bias, np.float32(NEG_BIG))
    return out


def _a_attn(qkvs, seq):
    b = qkvs[0].shape[0]
    in_specs, args = [], []
    for qkv, dil in zip(qkvs, A_DILS):
        for n in range(3):
            in_specs.append(pl.BlockSpec((1, 1, 1, dil, seq // dil, A_HEAD_DIM),
                                         lambda bb, hh, n=n: (bb, n, hh, 0, 0, 0)))
            args.append(qkv)
    in_specs.append(pl.BlockSpec((1, 3 * len(A_GROUPS), A_QSUB, A_KWIN),
                                 lambda bb, hh: (hh, 0, 0, 0)))
    args.append(jnp.asarray(_alibi_bias_tables()))
    return pl.pallas_call(
        _a_attn_kernel,
        grid=(b, A_HEADS),
        in_specs=in_specs,
        out_specs=pl.BlockSpec((1, seq, A_HEAD_DIM), lambda bb, hh: (bb, 0, hh)),
        out_shape=jax.ShapeDtypeStruct((b, seq, A_GROUP_WIDTH), F32),
        scratch_shapes=[pltpu.VMEM((seq, LANES), F32), pltpu.VMEM((seq, LANES), F32)],
        compiler_params=_params("parallel", "arbitrary"),
        name="a_attn",
    )(*args)


def _dilated_layer(x, g, w_in_bf):
    qkv0, qkv1, qkv2, gate = _a_proj(x, g, w_in_bf)
    return _a_attn((qkv0, qkv1, qkv2), x.shape[1]), gate


def _rope(t, ca, cm, cp):
    return t * ca + pltpu.roll(t, 128 - B_ROPE // 2, 1) * cm + pltpu.roll(t, B_ROPE // 2, 1) * cp


def _b_proj_kernel(x_ref, o_ref, pgate_ref, wprev_ref, g_ref, w1_ref, qg_ref, wq_ref, kvg_ref,
                   wk_ref, wvt_ref, ca_ref, cm_ref, cp_ref,
                   xnew_ref, q_ref, k_ref, vt_ref, gate_ref):
    y = (o_ref[0] * _silu(pgate_ref[0])).astype(BF16)
    x = x_ref[0] + jnp.dot(y, wprev_ref[...], preferred_element_type=F32)
    xnew_ref[0] = x
    h = _rms(x, g_ref[...]).astype(BF16)
    proj = jnp.dot(h, w1_ref[...], preferred_element_type=F32)
    c_q = proj[:, :B_Q_LORA]
    c_kv = proj[:, B_Q_LORA:B_Q_LORA + B_KV_LORA]
    k_r = proj[:, B_Q_LORA + B_KV_LORA:B_Q_LORA + B_KV_LORA + B_QK_PAD]
    gate_ref[0] = proj[:, B_Q_LORA + B_KV_LORA + B_QK_PAD:]
    ca, cm, cp = ca_ref[...], cm_ref[...], cp_ref[...]

    qn = _rms(c_q, qg_ref[...]).astype(BF16)
    q = jnp.dot(qn, wq_ref[...], preferred_element_type=F32)
    for hh in range(B_HEADS):
        cs = slice(hh * B_QK_PAD, (hh + 1) * B_QK_PAD)
        q_ref[0, :, cs] = (_rope(q[:, cs], ca, cm, cp) * B_Q_PRESCALE).astype(BF16)

    kvn = _rms(c_kv, kvg_ref[...]).astype(BF16)
    kn = jnp.dot(kvn, wk_ref[...], preferred_element_type=F32)
    k_rot = _rope(k_r, ca, cm, cp)
    for hh in range(B_HEADS):
        cs = slice(hh * B_QK_PAD, (hh + 1) * B_QK_PAD)
        k_ref[0, :, cs] = (kn[:, cs] + k_rot).astype(BF16)
    vt_ref[0] = _dot_nt(wvt_ref[...], kvn).astype(BF16)


def _rope_coeffs(s):
    half = B_ROPE // 2
    inv_freq = (ROPE_THETA ** (-np.arange(0, B_ROPE, 2) / B_ROPE)).astype(np.float32)
    ang = jnp.arange(s, dtype=F32)[:, None] * jnp.asarray(inv_freq)[None, :]
    cos, sin = jnp.cos(ang), jnp.sin(ang)
    ones = jnp.ones((s, B_NOPE), F32)
    zn = jnp.zeros((s, B_NOPE), F32)
    zh = jnp.zeros((s, half), F32)
    zt = jnp.zeros((s, B_QK_PAD - B_NOPE - B_ROPE), F32)
    ca = jnp.concatenate([ones, cos, cos, zt], axis=1)
    cm = jnp.concatenate([zn, -sin, zh, zt], axis=1)
    cp = jnp.concatenate([zn, zh, sin, zt], axis=1)
    return ca, cm, cp


def _b_proj(x, o_prev, gate_prev, w_out_prev, g, w1, qg, wq, kvg, wk, wvt, coeffs, tm=512):
    b, s, _ = x.shape
    n1 = w1.shape[1]
    const = lambda bb, ii: (0, 0)
    tok = lambda bb, ii: (bb, ii, 0)
    qk_w = B_HEADS * B_QK_PAD
    v_w = B_HEADS * B_V
    return pl.pallas_call(
        _b_proj_kernel,
        grid=(b, s // tm),
        in_specs=[
            pl.BlockSpec((1, tm, D_MODEL), tok),
            pl.BlockSpec((1, tm, A_GROUP_WIDTH), tok),
            pl.BlockSpec((1, tm, A_GROUP_WIDTH), tok),
            pl.BlockSpec((A_GROUP_WIDTH, D_MODEL), const),
            pl.BlockSpec((1, D_MODEL), const),
            pl.BlockSpec((D_MODEL, n1), const),
            pl.BlockSpec((1, B_Q_LORA), const),
            pl.BlockSpec((B_Q_LORA, qk_w), const),
            pl.BlockSpec((1, B_KV_LORA), const),
            pl.BlockSpec((B_KV_LORA, qk_w), const),
            pl.BlockSpec((v_w, B_KV_LORA), const),
            pl.BlockSpec((tm, B_QK_PAD), lambda bb, ii: (ii, 0)),
            pl.BlockSpec((tm, B_QK_PAD), lambda bb, ii: (ii, 0)),
            pl.BlockSpec((tm, B_QK_PAD), lambda bb, ii: (ii, 0)),
        ],
        out_specs=[
            pl.BlockSpec((1, tm, D_MODEL), tok),
            pl.BlockSpec((1, tm, qk_w), tok),
            pl.BlockSpec((1, tm, qk_w), tok),
            pl.BlockSpec((1, v_w, tm), lambda bb, ii: (bb, 0, ii)),
            pl.BlockSpec((1, tm, v_w), tok),
        ],
        out_shape=[
            jax.ShapeDtypeStruct((b, s, D_MODEL), F32),
            jax.ShapeDtypeStruct((b, s, qk_w), BF16),
            jax.ShapeDtypeStruct((b, s, qk_w), BF16),
            jax.ShapeDtypeStruct((b, v_w, s), BF16),
            jax.ShapeDtypeStruct((b, s, v_w), F32),
        ],
        compiler_params=_params("parallel", "parallel"),
        name="b_proj",
    )(x, o_prev, gate_prev, w_out_prev, g, w1, qg, wq, kvg, wk, wvt, *coeffs)


def _mla_attn_kernel(q_ref, k_ref, vt_ref, gate_ref, y_ref, st_scr, *, qb, tk, span):
    s = k_ref.shape[1]
    nq = s // qb
    nspan = s // (span * tk)

    def keys_of(c):
        return pl.ds(pl.multiple_of(c * tk, tk), tk)

    def queries_of(qi):
        return pl.ds(pl.multiple_of(qi * qb, qb), qb)

    def scores(qi, sp, slot):
        keys = pl.ds(pl.multiple_of(sp * (span * tk), span * tk), span * tk)
        for hd in range(2):
            cs = slice(hd * B_QK_PAD, (hd + 1) * B_QK_PAD)
            st_scr[slot, hd] = _dot_nt(k_ref[0, keys, cs], q_ref[0, queries_of(qi), cs])

    ones_rows = jnp.ones((B_ONES_ROWS, tk), BF16)

    def update(c, slot, part, carry):
        new = []
        for hd in range(2):
            m, acc = carry[hd]
            vrows = slice(hd * B_V, (hd + 1) * B_V)
            st = st_scr[slot, hd, part * tk:(part + 1) * tk, :]
            mn = jnp.maximum(m, jnp.max(st, axis=0, keepdims=True))
            alpha = jnp.exp2(m - mn)
            p = jnp.exp2(st - mn).astype(BF16)
            v1 = jnp.concatenate([vt_ref[0, vrows, keys_of(c)], ones_rows], axis=0)
            acc = acc * alpha + jnp.dot(v1, p, preferred_element_type=F32)
            new.append((mn, acc))
        return tuple(new)

    def qblock(qi, _):
        init = (jnp.full((1, qb), NEG_BIG, F32), jnp.zeros((B_V + B_ONES_ROWS, qb), F32))
        carry = (init, init)
        for sp in range(nspan):
            if sp + 1 < nspan:
                scores(qi, sp + 1, (sp + 1) % 2)
            else:
                scores(jnp.minimum(qi + 1, nq - 1), 0, (sp + 1) % 2)
            for part in range(span):
                carry = update(sp * span + part, sp % 2, part, carry)
        outs = [acc[:B_V] / acc[B_V:B_V + 1] for _, acc in carry]
        o = jnp.concatenate(outs, axis=0).T
        rows = queries_of(qi)
        y_ref[0, rows, :] = (o * _silu(gate_ref[0, rows, :])).astype(BF16)
        return 0

    scores(0, 0, 0)
    lax.fori_loop(0, nq, qblock, 0, unroll=4)


def _mla_attn(q, k, vt, gate, qb=256, tk=256, span=2):
    b, s, _ = q.shape
    assert (s // (span * tk)) % 2 == 0
    rows = lambda bb, hp: (bb, 0, hp)
    return pl.pallas_call(
        functools.partial(_mla_attn_kernel, qb=qb, tk=tk, span=span),
        grid=(b, B_HEADS // 2),
        in_specs=[
            pl.BlockSpec((1, s, 2 * B_QK_PAD), rows),
            pl.BlockSpec((1, s, 2 * B_QK_PAD), rows),
            pl.BlockSpec((1, 2 * B_V, s), lambda bb, hp: (bb, hp, 0)),
            pl.BlockSpec((1, s, 2 * B_V), rows),
        ],
        out_specs=pl.BlockSpec((1, s, 2 * B_V), rows),
        out_shape=jax.ShapeDtypeStruct((b, s, B_HEADS * B_V), BF16),
        scratch_shapes=[pltpu.VMEM((2, 2, span * tk, qb), F32)],
        compiler_params=_params("parallel", "arbitrary"),
        name="b_attn",
    )(q, k, vt, gate)


def _out_proj_kernel(x_ref, y_ref, w_ref, fg_ref, o_ref, *, final_norm):
    xn = x_ref[...] + jnp.dot(y_ref[...], w_ref[...], preferred_element_type=F32)
    if final_norm:
        xn = _rms(xn, fg_ref[...])
    o_ref[...] = xn


def _out_proj(x2, y2, w_bf, fg, final_norm, tm=1024):
    t = x2.shape[0]
    tok = pl.BlockSpec((tm, D_MODEL), lambda i: (i, 0))
    return pl.pallas_call(
        functools.partial(_out_proj_kernel, final_norm=final_norm),
        grid=(t // tm,),
        in_specs=[tok, tok,
                  pl.BlockSpec((D_MODEL, D_MODEL), lambda i: (0, 0)),
                  pl.BlockSpec((1, D_MODEL), lambda i: (0, 0))],
        out_specs=tok,
        out_shape=jax.ShapeDtypeStruct((t, D_MODEL), F32),
        compiler_params=_params("parallel"),
        name="out_proj",
    )(x2, y2, w_bf, fg)


def _mla_layer(x, o_prev, gate_prev, w_out_prev, g, wts, coeffs, fg, final_norm):
    b, s, _ = x.shape
    w1, qg, wq, kvg, wk, wvt, w_out = wts
    x, q, k, vt, gate = _b_proj(x, o_prev, gate_prev, w_out_prev, g, w1, qg, wq, kvg, wk, wvt,
                                coeffs)
    y = _mla_attn(q, k, vt, gate)
    out = _out_proj(x.reshape(b * s, D_MODEL), y.reshape(b * s, D_MODEL), w_out, fg, final_norm)
    return out.reshape(b, s, D_MODEL)


def _prep_mla_weights(w_in, q_g, w_q_up, kv_g, w_kv_up, w_out):
    c0, c1, c2 = B_Q_LORA, B_Q_LORA + B_KV_LORA, B_Q_LORA + B_KV_LORA + B_ROPE
    pad_r = B_QK_PAD - B_NOPE - B_ROPE
    w_kr = jnp.pad(w_in[:, c1:c2], ((0, 0), (B_NOPE, pad_r)))
    w1 = jnp.concatenate([w_in[:, :c1], w_kr, w_in[:, c2:]], axis=1).astype(BF16)
    wq = w_q_up.reshape(B_Q_LORA, B_HEADS, B_NOPE + B_ROPE)
    wq = jnp.pad(wq, ((0, 0), (0, 0), (0, pad_r))).reshape(B_Q_LORA, B_HEADS * B_QK_PAD)
    wkv = w_kv_up.reshape(B_KV_LORA, B_HEADS, B_NOPE + B_V)
    wk = jnp.pad(wkv[:, :, :B_NOPE], ((0, 0), (0, 0), (0, B_QK_PAD - B_NOPE)))
    wk = wk.reshape(B_KV_LORA, B_HEADS * B_QK_PAD)
    wvt = wkv[:, :, B_NOPE:].reshape(B_KV_LORA, B_HEADS * B_V).T
    return (w1, q_g.reshape(1, -1), wq.astype(BF16), kv_g.reshape(1, -1),
            wk.astype(BF16), wvt.astype(BF16), w_out.astype(BF16))


def _trunk(x, norm_g, a_wts, b_wts, coeffs, fg):
    for i in range(DEPTH):
        g = norm_g[i].reshape(1, D_MODEL)
        j = i // 2
        if i % 2 == 0:
            w_in_a, w_out_a = a_wts[j]
            o_a, gate_a = _dilated_layer(x, g, w_in_a)
        else:
            x = _mla_layer(x, o_a, gate_a, w_out_a, g, b_wts[j], coeffs, fg,
                           final_norm=(i == DEPTH - 1))
    return x


def kernel(x_prompt, x_sample, norm_g, a_w_in, a_w_out, b_w_in, b_q_norm_g, b_w_q_up,
           b_kv_norm_g, b_w_kv_up, b_w_out, final_norm_g):
    a_wts = [(a_w_in[j].astype(BF16).reshape(D_MODEL, A_NQKV + 1, A_GROUP_WIDTH).transpose(1, 0, 2),
              a_w_out[j].astype(BF16)) for j in range(a_w_in.shape[0])]
    b_wts = [_prep_mla_weights(b_w_in[j], b_q_norm_g[j], b_w_q_up[j], b_kv_norm_g[j],
                               b_w_kv_up[j], b_w_out[j]) for j in range(b_w_in.shape[0])]
    fg = final_norm_g.reshape(1, D_MODEL)
    outs = []
    for x in (x_prompt, x_sample):
        coeffs = _rope_coeffs(x.shape[1])
        outs.append(_trunk(x, norm_g, a_wts, b_wts, coeffs, fg))
    return tuple(outs)
```

```python
import functools

import numpy as np
import jax
import jax.numpy as jnp
from jax import lax
from jax.experimental import pallas as pl
from jax.experimental.pallas import tpu as pltpu

F32 = jnp.float32
BF16 = jnp.bfloat16

D_MODEL = 1024
DEPTH = 4
RMS_EPS = 1e-6
NEG_BIG = -1e30
LOG2E = float(np.log2(np.e))
LANES = 128

A_GROUPS = ((128, 1), (512, 4), (2048, 16))
A_DILS = tuple(d for _, d in A_GROUPS)
A_HEADS = 8
A_HEAD_DIM = 128
A_GROUP_WIDTH = A_HEADS * A_HEAD_DIM
A_NQKV = 3 * len(A_GROUPS)
ALIBI_MAX_EXP = 8.0
A_RADIUS = 64
A_QSUB = 128
A_KWIN = A_QSUB + 2 * A_RADIUS
A_TILES_PER_ITER = 8
A_SCORE_AHEAD = 2
A_MERGE_BEHIND = 2
A_Q_PRESCALE = float(LOG2E * A_HEAD_DIM ** -0.5)

B_HEADS = 16
B_NOPE = 64
B_ROPE = 32
B_V = 64
B_Q_LORA = 384
B_KV_LORA = 256
B_QK_PAD = 128
B_ONES_ROWS = 16
B_SCORE_AHEAD = 1
B_SCORE_SLOTS = 2
ROPE_THETA = 10000.0
B_Q_PRESCALE = float(LOG2E * (B_NOPE + B_ROPE) ** -0.5)

VMEM_LIMIT = 56 * 1024 * 1024


def _params(*sem):
    return pltpu.CompilerParams(dimension_semantics=sem, vmem_limit_bytes=VMEM_LIMIT)


def _rms(xf, g):
    ms = jnp.mean(xf * xf, axis=-1, keepdims=True)
    return xf * lax.rsqrt(ms + RMS_EPS) * g


def _silu(g):
    return g * (1.0 / (1.0 + jnp.exp(-g)))


def _dot_nt(a, b):
    return lax.dot_general(a, b, (((1,), (1,)), ((), ())), preferred_element_type=F32)


def _a_proj_kernel(x_ref, g_ref, w_ref, qkv0_ref, qkv1_ref, qkv2_ref, gate_ref,
                   h_scr, h1_scr, h2_scr, hf_scr, hfm_scr):
    tm = x_ref.shape[1]
    nslab = D_MODEL // LANES
    ratio = A_DILS[2] // A_DILS[1]
    assert A_DILS[1] == ratio
    n_mid = tm // A_DILS[1]
    n_wide = tm // A_DILS[2]

    h = _rms(x_ref[0], g_ref[...])
    h_scr[...] = h.astype(BF16)
    for c in range(nslab):
        hf_scr[c * tm:(c + 1) * tm, :] = h[:, c * LANES:(c + 1) * LANES]

    def permute_mid(r):
        for c in range(nslab):
            piece = hf_scr[pl.ds(c * tm + r, n_mid, stride=A_DILS[1]), :]
            hfm_scr[c * tm + r * n_mid:c * tm + (r + 1) * n_mid, :] = piece
            h1_scr[r * n_mid:(r + 1) * n_mid, c * LANES:(c + 1) * LANES] = piece.astype(BF16)

    def permute_wide(r):
        r_mid, a = r % A_DILS[1], r // A_DILS[1]
        for c in range(nslab):
            piece = hfm_scr[pl.ds(c * tm + r_mid * n_mid + a, n_wide, stride=ratio), :]
            h2_scr[r * n_wide:(r + 1) * n_wide, c * LANES:(c + 1) * LANES] = piece.astype(BF16)

    def project(lhs_scr, m):
        acc = jnp.dot(lhs_scr[...], w_ref[m], preferred_element_type=F32)
        return acc * A_Q_PRESCALE if m % 3 == 0 else acc

    def head_cols(hh):
        return slice(hh * A_HEAD_DIM, (hh + 1) * A_HEAD_DIM)

    for m in range(3):
        acc = project(h_scr, m).astype(BF16)
        for hh in range(A_HEADS):
            qkv0_ref[0, m, hh, 0] = acc[:, head_cols(hh)]
        if m < 2:
            for r in range(m * 2, m * 2 + 2):
                permute_mid(r)
    gate_ref[0] = jnp.dot(h_scr[...], w_ref[A_NQKV], preferred_element_type=F32)
    for m in range(3, 6):
        acc = project(h1_scr, m).astype(BF16)
        for r in range(A_DILS[1]):
            for hh in range(A_HEADS):
                qkv1_ref[0, m - 3, hh, r] = acc[r * n_mid:(r + 1) * n_mid, head_cols(hh)]
        if m == 3:
            for r in range(A_DILS[2]):
                permute_wide(r)
    for m in range(6, 9):
        acc = project(h2_scr, m).astype(BF16)
        for r in range(A_DILS[2]):
            for hh in range(A_HEADS):
                qkv2_ref[0, m - 6, hh, r] = acc[r * n_wide:(r + 1) * n_wide, head_cols(hh)]


def _a_proj(x, g, w_bf, tm=512):
    b, s, _ = x.shape
    gw = A_GROUP_WIDTH
    tok = lambda bb, ii: (bb, ii, 0)
    return pl.pallas_call(
        _a_proj_kernel,
        grid=(b, s // tm),
        in_specs=[
            pl.BlockSpec((1, tm, D_MODEL), tok),
            pl.BlockSpec((1, D_MODEL), lambda bb, ii: (0, 0)),
            pl.BlockSpec((A_NQKV + 1, D_MODEL, gw), lambda bb, ii: (0, 0, 0),
                         pipeline_mode=pl.Buffered(1)),
        ],
        out_specs=[pl.BlockSpec((1, 3, A_HEADS, d, tm // d, A_HEAD_DIM),
                                lambda bb, ii: (bb, 0, 0, 0, ii, 0))
                   for d in A_DILS] + [pl.BlockSpec((1, tm, gw), tok)],
        out_shape=[jax.ShapeDtypeStruct((b, 3, A_HEADS, d, s // d, A_HEAD_DIM), BF16)
                   for d in A_DILS] + [jax.ShapeDtypeStruct((b, s, gw), F32)],
        scratch_shapes=[pltpu.VMEM((tm, D_MODEL), BF16),
                        pltpu.VMEM((tm, D_MODEL), BF16),
                        pltpu.VMEM((tm, D_MODEL), BF16),
                        pltpu.VMEM((D_MODEL // LANES * tm, LANES), F32),
                        pltpu.VMEM((D_MODEL // LANES * tm, LANES), F32)],
        compiler_params=_params("parallel", "parallel"),
        name="a_proj",
    )(x, g, w_bf)


def _a_attn_kernel(q0, k0, v0, q1, k1, v1, q2, k2, v2, bias_ref, out_ref, m_scr, l_scr):
    o_ref = out_ref.at[0]
    seq = o_ref.shape[0]
    ntile = seq // A_QSUB
    ones_cols = jnp.ones((A_KWIN, LANES), BF16)

    def place(dil, t):
        seq_l = seq // dil
        nsub = seq_l // A_QSUB
        r = t // nsub
        sub = t % nsub
        row0 = pl.multiple_of(sub * A_QSUB, A_QSUB)
        start = pl.multiple_of(jnp.clip(row0 - A_RADIUS, 0, seq_l - A_KWIN), A_RADIUS)
        case = jnp.where(sub == 0, 0, jnp.where(sub == nsub - 1, 2, 1))
        if dil == 1:
            rows = pl.ds(row0, A_QSUB)
        else:
            rows = pl.ds(row0 * dil + r, A_QSUB, stride=dil)
        return r, row0, start, case, rows

    first = len(A_GROUPS) - 1
    for group, refs in reversed(tuple(enumerate(((q0, k0, v0), (q1, k1, v1), (q2, k2, v2))))):
        dil = A_DILS[group]

        def step(it, carry, group=group, refs=refs, dil=dil):
            q_ref, k_ref, v_ref = refs
            n = A_TILES_PER_ITER
            places = [place(dil, it * n + u) for u in range(n)]
            scs, stats = {}, {}

            def score(u):
                r, row0, start, case, _ = places[u]
                scs[u] = (_dot_nt(q_ref[0, 0, 0, r, pl.ds(row0, A_QSUB), :],
                                  k_ref[0, 0, 0, r, pl.ds(start, A_KWIN), :])
                          + bias_ref[0, 3 * group + case])

            def softmax_values(u):
                r, _, start, _, rows = places[u]
                sc = scs.pop(u)
                m = jnp.max(sc, axis=-1, keepdims=True)
                if group == first:
                    m_old = None
                    m = jnp.broadcast_to(m, (A_QSUB, LANES))
                else:
                    m_old = m_scr[rows, :]
                    m = jnp.maximum(m_old, m)
                p = jnp.exp2(sc - jnp.concatenate([m, m], axis=1)).astype(BF16)
                v1 = jnp.concatenate([v_ref[0, 0, 0, r, pl.ds(start, A_KWIN), :], ones_cols],
                                     axis=1)
                acc = jnp.dot(p, v1, preferred_element_type=F32)
                stats[u] = (m_old, m, acc[:, A_HEAD_DIM:], acc[:, :A_HEAD_DIM])

            def merge(u):
                rows = places[u][4]
                m_old, m, l, acc = stats.pop(u)
                m_scr[rows, :] = m
                if group == first:
                    l_scr[rows, :] = l
                    o_ref[rows, :] = acc
                else:
                    a_old = jnp.exp2(m_old - m)
                    l_scr[rows, :] = l_scr[rows, :] * a_old + l
                    o_ref[rows, :] = o_ref[rows, :] * a_old + acc

            for s in range(n + A_SCORE_AHEAD + A_MERGE_BEHIND):
                if s < n:
                    score(s)
                if 0 <= s - A_SCORE_AHEAD < n:
                    softmax_values(s - A_SCORE_AHEAD)
                if 0 <= s - A_SCORE_AHEAD - A_MERGE_BEHIND < n:
                    merge(s - A_SCORE_AHEAD - A_MERGE_BEHIND)
            return carry

        lax.fori_loop(0, ntile // A_TILES_PER_ITER, step, 0, unroll=True)

    o_ref[...] = o_ref[...] / l_scr[...]


def _alibi_bias_tables():
    n = len(A_GROUPS) * A_HEADS
    slopes = (2.0 ** (-ALIBI_MAX_EXP * np.arange(1, n + 1) / n)).astype(np.float32)
    slopes = slopes.reshape(len(A_GROUPS), A_HEADS)
    i = np.arange(A_QSUB)[:, None]
    j = np.arange(A_KWIN)[None, :]
    out = np.empty((A_HEADS, 3 * len(A_GROUPS), A_QSUB, A_KWIN), np.float32)
    for g, dil in enumerate(A_DILS):
        for case, shift in enumerate((0, -A_RADIUS, -2 * A_RADIUS)):
            delta = np.abs(j - i + shift)
            bias = (-slopes[g][:, None, None] * (delta * dil)[None]).astype(np.float32)
            bias = bias * np.float32(LOG2E)
            out[:, 3 * g + case] = np.where((delta <= A_RADIUS)[None], bias, np.float32(NEG_BIG))
    return out


def _a_attn(qkvs, seq):
    b = qkvs[0].shape[0]
    in_specs, args = [], []
    for qkv, dil in zip(qkvs, A_DILS):
        for n in range(3):
            in_specs.append(pl.BlockSpec((1, 1, 1, dil, seq // dil, A_HEAD_DIM),
                                         lambda bb, hh, n=n: (bb, n, hh, 0, 0, 0)))
            args.append(qkv)
    in_specs.append(pl.BlockSpec((1, 3 * len(A_GROUPS), A_QSUB, A_KWIN),
                                 lambda bb, hh: (hh, 0, 0, 0)))
    args.append(jnp.asarray(_alibi_bias_tables()))
    return pl.pallas_call(
        _a_attn_kernel,
        grid=(b, A_HEADS),
        in_specs=in_specs,
        out_specs=pl.BlockSpec((1, seq, A_HEAD_DIM), lambda bb, hh: (bb, 0, hh)),
        out_shape=jax.ShapeDtypeStruct((b, seq, A_GROUP_WIDTH), F32),
        scratch_shapes=[pltpu.VMEM((seq, LANES), F32), pltpu.VMEM((seq, LANES), F32)],
        compiler_params=_params("parallel", "arbitrary"),
        name="a_attn",
    )(*args)


def _dilated_layer(x, g, w_in_bf):
    qkv0, qkv1, qkv2, gate = _a_proj(x, g, w_in_bf)
    return _a_attn((qkv0, qkv1, qkv2), x.shape[1]), gate


def _rope(t, ca, cm, cp):
    return t * ca + pltpu.roll(t, 128 - B_ROPE // 2, 1) * cm + pltpu.roll(t, B_ROPE // 2, 1) * cp


def _b_proj_kernel(x_ref, o_ref, pgate_ref, wprev_ref, g_ref, w1_ref, qg_ref, wq_ref, kvg_ref,
                   wk_ref, wvt_ref, ca_ref, cm_ref, cp_ref,
                   xnew_ref, q_ref, k_ref, vt_ref, gate_ref):
    y = (o_ref[0] * _silu(pgate_ref[0])).astype(BF16)
    x = x_ref[0] + jnp.dot(y, wprev_ref[...], preferred_element_type=F32)
    xnew_ref[0] = x
    h = _rms(x, g_ref[...]).astype(BF16)
    proj = jnp.dot(h, w1_ref[...], preferred_element_type=F32)
    c_q = proj[:, :B_Q_LORA]
    c_kv = proj[:, B_Q_LORA:B_Q_LORA + B_KV_LORA]
    k_r = proj[:, B_Q_LORA + B_KV_LORA:B_Q_LORA + B_KV_LORA + B_QK_PAD]
    gate_ref[0] = proj[:, B_Q_LORA + B_KV_LORA + B_QK_PAD:]
    ca, cm, cp = ca_ref[...], cm_ref[...], cp_ref[...]

    qn = _rms(c_q, qg_ref[...]).astype(BF16)
    q = jnp.dot(qn, wq_ref[...], preferred_element_type=F32)
    for hh in range(B_HEADS):
        cs = slice(hh * B_QK_PAD, (hh + 1) * B_QK_PAD)
        q_ref[0, :, cs] = (_rope(q[:, cs], ca, cm, cp) * B_Q_PRESCALE).astype(BF16)

    kvn = _rms(c_kv, kvg_ref[...]).astype(BF16)
    kn = jnp.dot(kvn, wk_ref[...], preferred_element_type=F32)
    k_rot = _rope(k_r, ca, cm, cp)
    for hh in range(B_HEADS):
        cs = slice(hh * B_QK_PAD, (hh + 1) * B_QK_PAD)
        k_ref[0, :, cs] = (kn[:, cs] + k_rot).astype(BF16)
    vt_ref[0] = _dot_nt(wvt_ref[...], kvn).astype(BF16)


def _rope_coeffs(s):
    half = B_ROPE // 2
    inv_freq = (ROPE_THETA ** (-np.arange(0, B_ROPE, 2) / B_ROPE)).astype(np.float32)
    ang = jnp.arange(s, dtype=F32)[:, None] * jnp.asarray(inv_freq)[None, :]
    cos, sin = jnp.cos(ang), jnp.sin(ang)
    ones = jnp.ones((s, B_NOPE), F32)
    zn = jnp.zeros((s, B_NOPE), F32)
    zh = jnp.zeros((s, half), F32)
    zt = jnp.zeros((s, B_QK_PAD - B_NOPE - B_ROPE), F32)
    ca = jnp.concatenate([ones, cos, cos, zt], axis=1)
    cm = jnp.concatenate([zn, -sin, zh, zt], axis=1)
    cp = jnp.concatenate([zn, zh, sin, zt], axis=1)
    return ca, cm, cp


def _b_proj(x, o_prev, gate_prev, w_out_prev, g, w1, qg, wq, kvg, wk, wvt, coeffs, tm=512):
    b, s, _ = x.shape
    n1 = w1.shape[1]
    const = lambda bb, ii: (0, 0)
    tok = lambda bb, ii: (bb, ii, 0)
    qk_w = B_HEADS * B_QK_PAD
    v_w = B_HEADS * B_V
    return pl.pallas_call(
        _b_proj_kernel,
        grid=(b, s // tm),
        in_specs=[
            pl.BlockSpec((1, tm, D_MODEL), tok),
            pl.BlockSpec((1, tm, A_GROUP_WIDTH), tok),
            pl.BlockSpec((1, tm, A_GROUP_WIDTH), tok),
            pl.BlockSpec((A_GROUP_WIDTH, D_MODEL), const),
            pl.BlockSpec((1, D_MODEL), const),
            pl.BlockSpec((D_MODEL, n1), const),
            pl.BlockSpec((1, B_Q_LORA), const),
            pl.BlockSpec((B_Q_LORA, qk_w), const),
            pl.BlockSpec((1, B_KV_LORA), const),
            pl.BlockSpec((B_KV_LORA, qk_w), const),
            pl.BlockSpec((v_w, B_KV_LORA), const),
            pl.BlockSpec((tm, B_QK_PAD), lambda bb, ii: (ii, 0)),
            pl.BlockSpec((tm, B_QK_PAD), lambda bb, ii: (ii, 0)),
            pl.BlockSpec((tm, B_QK_PAD), lambda bb, ii: (ii, 0)),
        ],
        out_specs=[
            pl.BlockSpec((1, tm, D_MODEL), tok),
            pl.BlockSpec((1, tm, qk_w), tok),
            pl.BlockSpec((1, tm, qk_w), tok),
            pl.BlockSpec((1, v_w, tm), lambda bb, ii: (bb, 0, ii)),
            pl.BlockSpec((1, tm, v_w), tok),
        ],
        out_shape=[
            jax.ShapeDtypeStruct((b, s, D_MODEL), F32),
            jax.ShapeDtypeStruct((b, s, qk_w), BF16),
            jax.ShapeDtypeStruct((b, s, qk_w), BF16),
            jax.ShapeDtypeStruct((b, v_w, s), BF16),
            jax.ShapeDtypeStruct((b, s, v_w), F32),
        ],
        compiler_params=_params("parallel", "parallel"),
        name="b_proj",
    )(x, o_prev, gate_prev, w_out_prev, g, w1, qg, wq, kvg, wk, wvt, *coeffs)


def _mla_attn_kernel(q_ref, k_ref, vt_ref, gate_ref, y_ref, st_scr, *, qb, tk, span):
    s = k_ref.shape[1]
    nq = s // qb
    nspan = s // (span * tk)

    def keys_of(c):
        return pl.ds(pl.multiple_of(c * tk, tk), tk)

    def queries_of(qi):
        return pl.ds(pl.multiple_of(qi * qb, qb), qb)

    def scores(qi, sp, slot):
        keys = pl.ds(pl.multiple_of(sp * (span * tk), span * tk), span * tk)
        for hd in range(2):
            cs = slice(hd * B_QK_PAD, (hd + 1) * B_QK_PAD)
            st_scr[slot, hd] = _dot_nt(k_ref[0, keys, cs], q_ref[0, queries_of(qi), cs])

    ones_rows = jnp.ones((B_ONES_ROWS, tk), BF16)

    def update(c, slot, part, carry):
        new = []
        for hd in range(2):
            m, acc = carry[hd]
            vrows = slice(hd * B_V, (hd + 1) * B_V)
            st = st_scr[slot, hd, part * tk:(part + 1) * tk, :]
            mn = jnp.maximum(m, jnp.max(st, axis=0, keepdims=True))
            alpha = jnp.exp2(m - mn)
            p = jnp.exp2(st - mn).astype(BF16)
            v1 = jnp.concatenate([vt_ref[0, vrows, keys_of(c)], ones_rows], axis=0)
            acc = acc * alpha + jnp.dot(v1, p, preferred_element_type=F32)
            new.append((mn, acc))
        return tuple(new)

    def qblock(qi, _):
        init = (jnp.full((1, qb), NEG_BIG, F32), jnp.zeros((B_V + B_ONES_ROWS, qb), F32))
        carry = (init, init)
        for sp in range(nspan):
            nxt = sp + B_SCORE_AHEAD
            if nxt < nspan:
                scores(qi, nxt, nxt % B_SCORE_SLOTS)
            else:
                scores(jnp.minimum(qi + 1, nq - 1), nxt - nspan, nxt % B_SCORE_SLOTS)
            for part in range(span):
                carry = update(sp * span + part, sp % B_SCORE_SLOTS, part, carry)
        outs = [acc[:B_V] / acc[B_V:B_V + 1] for _, acc in carry]
        o = jnp.concatenate(outs, axis=0).T
        rows = queries_of(qi)
        y_ref[0, rows, :] = (o * _silu(gate_ref[0, rows, :])).astype(BF16)
        return 0

    for sp in range(B_SCORE_AHEAD):
        scores(0, sp, sp)
    lax.fori_loop(0, nq, qblock, 0, unroll=4)


def _mla_attn(q, k, vt, gate, qb=256, tk=256, span=2):
    b, s, _ = q.shape
    assert (s // (span * tk)) % B_SCORE_SLOTS == 0 and B_SCORE_AHEAD < B_SCORE_SLOTS
    rows = lambda bb, hp: (bb, 0, hp)
    return pl.pallas_call(
        functools.partial(_mla_attn_kernel, qb=qb, tk=tk, span=span),
        grid=(b, B_HEADS // 2),
        in_specs=[
            pl.BlockSpec((1, s, 2 * B_QK_PAD), rows),
            pl.BlockSpec((1, s, 2 * B_QK_PAD), rows),
            pl.BlockSpec((1, 2 * B_V, s), lambda bb, hp: (bb, hp, 0)),
            pl.BlockSpec((1, s, 2 * B_V), rows),
        ],
        out_specs=pl.BlockSpec((1, s, 2 * B_V), rows),
        out_shape=jax.ShapeDtypeStruct((b, s, B_HEADS * B_V), BF16),
        scratch_shapes=[pltpu.VMEM((B_SCORE_SLOTS, 2, span * tk, qb), F32)],
        compiler_params=_params("parallel", "arbitrary"),
        name="b_attn",
    )(q, k, vt, gate)


def _out_proj_kernel(x_ref, y_ref, w_ref, fg_ref, o_ref, *, final_norm):
    xn = x_ref[...] + jnp.dot(y_ref[...], w_ref[...], preferred_element_type=F32)
    if final_norm:
        xn = _rms(xn, fg_ref[...])
    o_ref[...] = xn


def _out_proj(x2, y2, w_bf, fg, final_norm, tm=1024):
    t = x2.shape[0]
    tok = pl.BlockSpec((tm, D_MODEL), lambda i: (i, 0))
    return pl.pallas_call(
        functools.partial(_out_proj_kernel, final_norm=final_norm),
        grid=(t // tm,),
        in_specs=[tok, tok,
                  pl.BlockSpec((D_MODEL, D_MODEL), lambda i: (0, 0)),
                  pl.BlockSpec((1, D_MODEL), lambda i: (0, 0))],
        out_specs=tok,
        out_shape=jax.ShapeDtypeStruct((t, D_MODEL), F32),
        compiler_params=_params("parallel"),
        name="out_proj",
    )(x2, y2, w_bf, fg)


def _mla_layer(x, o_prev, gate_prev, w_out_prev, g, wts, coeffs, fg, final_norm):
    b, s, _ = x.shape
    w1, qg, wq, kvg, wk, wvt, w_out = wts
    x, q, k, vt, gate = _b_proj(x, o_prev, gate_prev, w_out_prev, g, w1, qg, wq, kvg, wk, wvt,
                                coeffs)
    y = _mla_attn(q, k, vt, gate)
    out = _out_proj(x.reshape(b * s, D_MODEL), y.reshape(b * s, D_MODEL), w_out, fg, final_norm)
    return out.reshape(b, s, D_MODEL)


def _prep_mla_weights(w_in, q_g, w_q_up, kv_g, w_kv_up, w_out):
    c0, c1, c2 = B_Q_LORA, B_Q_LORA + B_KV_LORA, B_Q_LORA + B_KV_LORA + B_ROPE
    pad_r = B_QK_PAD - B_NOPE - B_ROPE
    w_kr = jnp.pad(w_in[:, c1:c2], ((0, 0), (B_NOPE, pad_r)))
    w1 = jnp.concatenate([w_in[:, :c1], w_kr, w_in[:, c2:]], axis=1).astype(BF16)
    wq = w_q_up.reshape(B_Q_LORA, B_HEADS, B_NOPE + B_ROPE)
    wq = jnp.pad(wq, ((0, 0), (0, 0), (0, pad_r))).reshape(B_Q_LORA, B_HEADS * B_QK_PAD)
    wkv = w_kv_up.reshape(B_KV_LORA, B_HEADS, B_NOPE + B_V)
    wk = jnp.pad(wkv[:, :, :B_NOPE], ((0, 0), (0, 0), (0, B_QK_PAD - B_NOPE)))
    wk = wk.reshape(B_KV_LORA, B_HEADS * B_QK_PAD)
    wvt = wkv[:, :, B_NOPE:].reshape(B_KV_LORA, B_HEADS * B_V).T
    return (w1, q_g.reshape(1, -1), wq.astype(BF16), kv_g.reshape(1, -1),
            wk.astype(BF16), wvt.astype(BF16), w_out.astype(BF16))


def _trunk(x, norm_g, a_wts, b_wts, coeffs, fg):
    for i in range(DEPTH):
        g = norm_g[i].reshape(1, D_MODEL)
        j = i // 2
        if i % 2 == 0:
            w_in_a, w_out_a = a_wts[j]
            o_a, gate_a = _dilated_layer(x, g, w_in_a)
        else:
            x = _mla_layer(x, o_a, gate_a, w_out_a, g, b_wts[j], coeffs, fg,
                           final_norm=(i == DEPTH - 1))
    return x


def kernel(x_prompt, x_sample, norm_g, a_w_in, a_w_out, b_w_in, b_q_norm_g, b_w_q_up,
           b_kv_norm_g, b_w_kv_up, b_w_out, final_norm_g):
    a_wts = [(a_w_in[j].astype(BF16).reshape(D_MODEL, A_NQKV + 1, A_GROUP_WIDTH).transpose(1, 0, 2),
              a_w_out[j].astype(BF16)) for j in range(a_w_in.shape[0])]
    b_wts = [_prep_mla_weights(b_w_in[j], b_q_norm_g[j], b_w_q_up[j], b_kv_norm_g[j],
                               b_w_kv_up[j], b_w_out[j]) for j in range(b_w_in.shape[0])]
    fg = final_norm_g.reshape(1, D_MODEL)
    outs = []
    for x in (x_prompt, x_sample):
        coeffs = _rope_coeffs(x.shape[1])
        outs.append(_trunk(x, norm_g, a_wts, b_wts, coeffs, fg))
    return tuple(outs)
```

```python
import functools

import numpy as np
import jax
import jax.numpy as jnp
from jax import lax
from jax.experimental import pallas as pl
from jax.experimental.pallas import tpu as pltpu

F32 = jnp.float32
BF16 = jnp.bfloat16

D_MODEL = 1024
DEPTH = 4
RMS_EPS = 1e-6
NEG_BIG = -1e30
LOG2E = float(np.log2(np.e))
LANES = 128

A_GROUPS = ((128, 1), (512, 4), (2048, 16))
A_DILS = tuple(d for _, d in A_GROUPS)
A_HEADS = 8
A_HEAD_DIM = 128
A_GROUP_WIDTH = A_HEADS * A_HEAD_DIM
A_NQKV = 3 * len(A_GROUPS)
ALIBI_MAX_EXP = 8.0
A_RADIUS = 64
A_QSUB = 128
A_KWIN = A_QSUB + 2 * A_RADIUS
A_TILES_PER_ITER = 32
A_SCORE_AHEAD = 2
A_MERGE_BEHIND = 2
A_Q_PRESCALE = float(LOG2E * A_HEAD_DIM ** -0.5)

B_HEADS = 16
B_NOPE = 64
B_ROPE = 32
B_V = 64
B_Q_LORA = 384
B_KV_LORA = 256
B_QK_PAD = 128
B_ONES_ROWS = 16
B_SCORE_AHEAD = 1
B_SCORE_SLOTS = 2
ROPE_THETA = 10000.0
B_Q_PRESCALE = float(LOG2E * (B_NOPE + B_ROPE) ** -0.5)

VMEM_LIMIT = 56 * 1024 * 1024


def _params(*sem):
    return pltpu.CompilerParams(dimension_semantics=sem, vmem_limit_bytes=VMEM_LIMIT)


def _rms(xf, g):
    ms = jnp.mean(xf * xf, axis=-1, keepdims=True)
    return xf * lax.rsqrt(ms + RMS_EPS) * g


def _silu(g):
    return g * (1.0 / (1.0 + jnp.exp(-g)))


def _dot_nt(a, b):
    return lax.dot_general(a, b, (((1,), (1,)), ((), ())), preferred_element_type=F32)


def _a_proj_kernel(x_ref, g_ref, w_ref, qkv0_ref, qkv1_ref, qkv2_ref, gate_ref,
                   h_scr, h1_scr, h2_scr, hf_scr, hfm_scr):
    tm = x_ref.shape[1]
    nslab = D_MODEL // LANES
    ratio = A_DILS[2] // A_DILS[1]
    assert A_DILS[1] == ratio
    n_mid = tm // A_DILS[1]
    n_wide = tm // A_DILS[2]

    h = _rms(x_ref[0], g_ref[...])
    h_scr[...] = h.astype(BF16)
    for c in range(nslab):
        hf_scr[c * tm:(c + 1) * tm, :] = h[:, c * LANES:(c + 1) * LANES]

    def permute_mid(r):
        for c in range(nslab):
            piece = hf_scr[pl.ds(c * tm + r, n_mid, stride=A_DILS[1]), :]
            hfm_scr[c * tm + r * n_mid:c * tm + (r + 1) * n_mid, :] = piece
            h1_scr[r * n_mid:(r + 1) * n_mid, c * LANES:(c + 1) * LANES] = piece.astype(BF16)

    def permute_wide(r):
        r_mid, a = r % A_DILS[1], r // A_DILS[1]
        for c in range(nslab):
            piece = hfm_scr[pl.ds(c * tm + r_mid * n_mid + a, n_wide, stride=ratio), :]
            h2_scr[r * n_wide:(r + 1) * n_wide, c * LANES:(c + 1) * LANES] = piece.astype(BF16)

    def project(lhs_scr, m):
        acc = jnp.dot(lhs_scr[...], w_ref[m], preferred_element_type=F32)
        return acc * A_Q_PRESCALE if m % 3 == 0 else acc

    def head_cols(hh):
        return slice(hh * A_HEAD_DIM, (hh + 1) * A_HEAD_DIM)

    for m in range(3):
        acc = project(h_scr, m).astype(BF16)
        for hh in range(A_HEADS):
            qkv0_ref[0, m, hh, 0] = acc[:, head_cols(hh)]
        if m < 2:
            for r in range(m * 2, m * 2 + 2):
                permute_mid(r)
    gate_ref[0] = jnp.dot(h_scr[...], w_ref[A_NQKV], preferred_element_type=F32)
    for m in range(3, 6):
        acc = project(h1_scr, m).astype(BF16)
        for r in range(A_DILS[1]):
            for hh in range(A_HEADS):
                qkv1_ref[0, m - 3, hh, r] = acc[r * n_mid:(r + 1) * n_mid, head_cols(hh)]
        if m == 3:
            for r in range(A_DILS[2]):
                permute_wide(r)
    for m in range(6, 9):
        acc = project(h2_scr, m).astype(BF16)
        for r in range(A_DILS[2]):
            for hh in range(A_HEADS):
                qkv2_ref[0, m - 6, hh, r] = acc[r * n_wide:(r + 1) * n_wide, head_cols(hh)]


def _a_proj(x, g, w_bf, tm=512):
    b, s, _ = x.shape
    gw = A_GROUP_WIDTH
    tok = lambda bb, ii: (bb, ii, 0)
    return pl.pallas_call(
        _a_proj_kernel,
        grid=(b, s // tm),
        in_specs=[
            pl.BlockSpec((1, tm, D_MODEL), tok),
            pl.BlockSpec((1, D_MODEL), lambda bb, ii: (0, 0)),
            pl.BlockSpec((A_NQKV + 1, D_MODEL, gw), lambda bb, ii: (0, 0, 0),
                         pipeline_mode=pl.Buffered(1)),
        ],
        out_specs=[pl.BlockSpec((1, 3, A_HEADS, d, tm // d, A_HEAD_DIM),
                                lambda bb, ii: (bb, 0, 0, 0, ii, 0))
                   for d in A_DILS] + [pl.BlockSpec((1, tm, gw), tok)],
        out_shape=[jax.ShapeDtypeStruct((b, 3, A_HEADS, d, s // d, A_HEAD_DIM), BF16)
                   for d in A_DILS] + [jax.ShapeDtypeStruct((b, s, gw), F32)],
        scratch_shapes=[pltpu.VMEM((tm, D_MODEL), BF16),
                        pltpu.VMEM((tm, D_MODEL), BF16),
                        pltpu.VMEM((tm, D_MODEL), BF16),
                        pltpu.VMEM((D_MODEL // LANES * tm, LANES), F32),
                        pltpu.VMEM((D_MODEL // LANES * tm, LANES), F32)],
        compiler_params=_params("parallel", "parallel"),
        name="a_proj",
    )(x, g, w_bf)


def _a_attn_kernel(q0, k0, v0, q1, k1, v1, q2, k2, v2, bias_ref, out_ref, m_scr, l_scr):
    o_ref = out_ref.at[0]
    seq = o_ref.shape[0]
    ntile = seq // A_QSUB
    ones_cols = jnp.ones((A_KWIN, LANES), BF16)

    def place(dil, t):
        seq_l = seq // dil
        nsub = seq_l // A_QSUB
        r = t // nsub
        sub = t % nsub
        row0 = pl.multiple_of(sub * A_QSUB, A_QSUB)
        start = pl.multiple_of(jnp.clip(row0 - A_RADIUS, 0, seq_l - A_KWIN), A_RADIUS)
        case = jnp.where(sub == 0, 0, jnp.where(sub == nsub - 1, 2, 1))
        if dil == 1:
            rows = pl.ds(row0, A_QSUB)
        else:
            rows = pl.ds(row0 * dil + r, A_QSUB, stride=dil)
        return r, row0, start, case, rows

    first = len(A_GROUPS) - 1
    for group, refs in reversed(tuple(enumerate(((q0, k0, v0), (q1, k1, v1), (q2, k2, v2))))):
        dil = A_DILS[group]

        def step(it, carry, group=group, refs=refs, dil=dil):
            q_ref, k_ref, v_ref = refs
            n = A_TILES_PER_ITER
            places = [place(dil, it * n + u) for u in range(n)]
            scs, stats = {}, {}

            def score(u):
                r, row0, start, case, _ = places[u]
                scs[u] = (_dot_nt(q_ref[0, 0, 0, r, pl.ds(row0, A_QSUB), :],
                                  k_ref[0, 0, 0, r, pl.ds(start, A_KWIN), :])
                          + bias_ref[0, 3 * group + case])

            def softmax_values(u):
                r, _, start, _, rows = places[u]
                sc = scs.pop(u)
                m = jnp.max(sc, axis=-1, keepdims=True)
                if group == first:
                    m_old = None
                    m = jnp.broadcast_to(m, (A_QSUB, LANES))
                else:
                    m_old = m_scr[rows, :]
                    m = jnp.maximum(m_old, m)
                p = jnp.exp2(sc - jnp.concatenate([m, m], axis=1)).astype(BF16)
                v1 = jnp.concatenate([v_ref[0, 0, 0, r, pl.ds(start, A_KWIN), :], ones_cols],
                                     axis=1)
                acc = jnp.dot(p, v1, preferred_element_type=F32)
                stats[u] = (m_old, m, acc[:, A_HEAD_DIM:], acc[:, :A_HEAD_DIM])

            def merge(u):
                rows = places[u][4]
                m_old, m, l, acc = stats.pop(u)
                m_scr[rows, :] = m
                if group == first:
                    l_scr[rows, :] = l
                    o_ref[rows, :] = acc
                else:
                    a_old = jnp.exp2(m_old - m)
                    l_scr[rows, :] = l_scr[rows, :] * a_old + l
                    o_ref[rows, :] = o_ref[rows, :] * a_old + acc

            for s in range(n + A_SCORE_AHEAD + A_MERGE_BEHIND):
                if s < n:
                    score(s)
                if 0 <= s - A_SCORE_AHEAD < n:
                    softmax_values(s - A_SCORE_AHEAD)
                if 0 <= s - A_SCORE_AHEAD - A_MERGE_BEHIND < n:
                    merge(s - A_SCORE_AHEAD - A_MERGE_BEHIND)
            return carry

        lax.fori_loop(0, ntile // A_TILES_PER_ITER, step, 0, unroll=True)

    o_ref[...] = o_ref[...] / l_scr[...]


def _alibi_bias_tables():
    n = len(A_GROUPS) * A_HEADS
    slopes = (2.0 ** (-ALIBI_MAX_EXP * np.arange(1, n + 1) / n)).astype(np.float32)
    slopes = slopes.reshape(len(A_GROUPS), A_HEADS)
    i = np.arange(A_QSUB)[:, None]
    j = np.arange(A_KWIN)[None, :]
    out = np.empty((A_HEADS, 3 * len(A_GROUPS), A_QSUB, A_KWIN), np.float32)
    for g, dil in enumerate(A_DILS):
        for case, shift in enumerate((0, -A_RADIUS, -2 * A_RADIUS)):
            delta = np.abs(j - i + shift)
            bias = (-slopes[g][:, None, None] * (delta * dil)[None]).astype(np.float32)
            bias = bias * np.float32(LOG2E)
            out[:, 3 * g + case] = np.where((delta <= A_RADIUS)[None], bias, np.float32(NEG_BIG))
    return out


def _a_attn(qkvs, seq):
    b = qkvs[0].shape[0]
    in_specs, args = [], []
    for qkv, dil in zip(qkvs, A_DILS):
        for n in range(3):
            in_specs.append(pl.BlockSpec((1, 1, 1, dil, seq // dil, A_HEAD_DIM),
                                         lambda bb, hh, n=n: (bb, n, hh, 0, 0, 0)))
            args.append(qkv)
    in_specs.append(pl.BlockSpec((1, 3 * len(A_GROUPS), A_QSUB, A_KWIN),
                                 lambda bb, hh: (hh, 0, 0, 0)))
    args.append(jnp.asarray(_alibi_bias_tables()))
    return pl.pallas_call(
        _a_attn_kernel,
        grid=(b, A_HEADS),
        in_specs=in_specs,
        out_specs=pl.BlockSpec((1, seq, A_HEAD_DIM), lambda bb, hh: (bb, 0, hh)),
        out_shape=jax.ShapeDtypeStruct((b, seq, A_GROUP_WIDTH), F32),
        scratch_shapes=[pltpu.VMEM((seq, LANES), F32), pltpu.VMEM((seq, LANES), F32)],
        compiler_params=_params("parallel", "arbitrary"),
        name="a_attn",
    )(*args)


def _dilated_layer(x, g, w_in_bf):
    qkv0, qkv1, qkv2, gate = _a_proj(x, g, w_in_bf)
    return _a_attn((qkv0, qkv1, qkv2), x.shape[1]), gate


def _rope(t, ca, cm, cp):
    half = B_ROPE // 2
    return t * ca + pltpu.roll(t, B_QK_PAD - half, 1) * cm + pltpu.roll(t, half, 1) * cp


def _b_proj_kernel(x_ref, o_ref, pgate_ref, wprev_ref, g_ref, w1_ref, qg_ref, wq_ref, kvg_ref,
                   wk_ref, wvt_ref, ca_ref, cm_ref, cp_ref,
                   xnew_ref, q_ref, k_ref, vt_ref, gate_ref):
    y = (o_ref[0] * _silu(pgate_ref[0])).astype(BF16)
    x = x_ref[0] + jnp.dot(y, wprev_ref[...], preferred_element_type=F32)
    xnew_ref[0] = x
    h = _rms(x, g_ref[...]).astype(BF16)
    proj = jnp.dot(h, w1_ref[...], preferred_element_type=F32)
    c_q = proj[:, :B_Q_LORA]
    c_kv = proj[:, B_Q_LORA:B_Q_LORA + B_KV_LORA]
    k_r = proj[:, B_Q_LORA + B_KV_LORA:B_Q_LORA + B_KV_LORA + B_QK_PAD]
    gate_ref[0] = proj[:, B_Q_LORA + B_KV_LORA + B_QK_PAD:]
    ca, cm, cp = ca_ref[...], cm_ref[...], cp_ref[...]

    qn = _rms(c_q, qg_ref[...]).astype(BF16)
    q = jnp.dot(qn, wq_ref[...], preferred_element_type=F32)
    for hh in range(B_HEADS):
        cs = slice(hh * B_QK_PAD, (hh + 1) * B_QK_PAD)
        q_ref[0, :, cs] = (_rope(q[:, cs], ca, cm, cp) * B_Q_PRESCALE).astype(BF16)

    kvn = _rms(c_kv, kvg_ref[...]).astype(BF16)
    kn = jnp.dot(kvn, wk_ref[...], preferred_element_type=F32)
    k_rot = _rope(k_r, ca, cm, cp)
    for hh in range(B_HEADS):
        cs = slice(hh * B_QK_PAD, (hh + 1) * B_QK_PAD)
        k_ref[0, :, cs] = (kn[:, cs] + k_rot).astype(BF16)
    vt_ref[0] = _dot_nt(wvt_ref[...], kvn).astype(BF16)


def _rope_coeffs(s):
    half = B_ROPE // 2
    inv_freq = (ROPE_THETA ** (-np.arange(0, B_ROPE, 2) / B_ROPE)).astype(np.float32)
    ang = jnp.arange(s, dtype=F32)[:, None] * jnp.asarray(inv_freq)[None, :]
    cos, sin = jnp.cos(ang), jnp.sin(ang)
    ones = jnp.ones((s, B_NOPE), F32)
    zn = jnp.zeros((s, B_NOPE), F32)
    zh = jnp.zeros((s, half), F32)
    zt = jnp.zeros((s, B_QK_PAD - B_NOPE - B_ROPE), F32)
    ca = jnp.concatenate([ones, cos, cos, zt], axis=1)
    cm = jnp.concatenate([zn, -sin, zh, zt], axis=1)
    cp = jnp.concatenate([zn, zh, sin, zt], axis=1)
    return ca, cm, cp


def _b_proj(x, o_prev, gate_prev, w_out_prev, g, w1, qg, wq, kvg, wk, wvt, coeffs, tm=512):
    b, s, _ = x.shape
    n1 = w1.shape[1]
    const = lambda bb, ii: (0, 0)
    tok = lambda bb, ii: (bb, ii, 0)
    qk_w = B_HEADS * B_QK_PAD
    v_w = B_HEADS * B_V
    return pl.pallas_call(
        _b_proj_kernel,
        grid=(b, s // tm),
        in_specs=[
            pl.BlockSpec((1, tm, D_MODEL), tok),
            pl.BlockSpec((1, tm, A_GROUP_WIDTH), tok),
            pl.BlockSpec((1, tm, A_GROUP_WIDTH), tok),
            pl.BlockSpec((A_GROUP_WIDTH, D_MODEL), const),
            pl.BlockSpec((1, D_MODEL), const),
            pl.BlockSpec((D_MODEL, n1), const),
            pl.BlockSpec((1, B_Q_LORA), const),
            pl.BlockSpec((B_Q_LORA, qk_w), const),
            pl.BlockSpec((1, B_KV_LORA), const),
            pl.BlockSpec((B_KV_LORA, qk_w), const),
            pl.BlockSpec((v_w, B_KV_LORA), const),
            pl.BlockSpec((tm, B_QK_PAD), lambda bb, ii: (ii, 0)),
            pl.BlockSpec((tm, B_QK_PAD), lambda bb, ii: (ii, 0)),
            pl.BlockSpec((tm, B_QK_PAD), lambda bb, ii: (ii, 0)),
        ],
        out_specs=[
            pl.BlockSpec((1, tm, D_MODEL), tok),
            pl.BlockSpec((1, tm, qk_w), tok),
            pl.BlockSpec((1, tm, qk_w), tok),
            pl.BlockSpec((1, v_w, tm), lambda bb, ii: (bb, 0, ii)),
            pl.BlockSpec((1, tm, v_w), tok),
        ],
        out_shape=[
            jax.ShapeDtypeStruct((b, s, D_MODEL), F32),
            jax.ShapeDtypeStruct((b, s, qk_w), BF16),
            jax.ShapeDtypeStruct((b, s, qk_w), BF16),
            jax.ShapeDtypeStruct((b, v_w, s), BF16),
            jax.ShapeDtypeStruct((b, s, v_w), F32),
        ],
        compiler_params=_params("parallel", "parallel"),
        name="b_proj",
    )(x, o_prev, gate_prev, w_out_prev, g, w1, qg, wq, kvg, wk, wvt, *coeffs)


def _mla_attn_kernel(q_ref, k_ref, vt_ref, gate_ref, y_ref, st_scr, *, qb, tk, span):
    s = k_ref.shape[1]
    nq = s // qb
    nspan = s // (span * tk)

    def keys_of(c):
        return pl.ds(pl.multiple_of(c * tk, tk), tk)

    def queries_of(qi):
        return pl.ds(pl.multiple_of(qi * qb, qb), qb)

    def scores(qi, sp, slot):
        keys = pl.ds(pl.multiple_of(sp * (span * tk), span * tk), span * tk)
        for hd in range(2):
            cs = slice(hd * B_QK_PAD, (hd + 1) * B_QK_PAD)
            st_scr[slot, hd] = _dot_nt(k_ref[0, keys, cs], q_ref[0, queries_of(qi), cs])

    ones_rows = jnp.ones((B_ONES_ROWS, tk), BF16)

    def update(c, slot, part, carry):
        new = []
        for hd in range(2):
            m, acc = carry[hd]
            vrows = slice(hd * B_V, (hd + 1) * B_V)
            st = st_scr[slot, hd, part * tk:(part + 1) * tk, :]
            mn = jnp.maximum(m, jnp.max(st, axis=0, keepdims=True))
            alpha = jnp.exp2(m - mn)
            p = jnp.exp2(st - mn).astype(BF16)
            v1 = jnp.concatenate([vt_ref[0, vrows, keys_of(c)], ones_rows], axis=0)
            acc = acc * alpha + jnp.dot(v1, p, preferred_element_type=F32)
            new.append((mn, acc))
        return tuple(new)

    def qblock(qi, _):
        init = (jnp.full((1, qb), NEG_BIG, F32), jnp.zeros((B_V + B_ONES_ROWS, qb), F32))
        carry = (init, init)
        for sp in range(nspan):
            nxt = sp + B_SCORE_AHEAD
            if nxt < nspan:
                scores(qi, nxt, nxt % B_SCORE_SLOTS)
            else:
                scores(jnp.minimum(qi + 1, nq - 1), nxt - nspan, nxt % B_SCORE_SLOTS)
            for part in range(span):
                carry = update(sp * span + part, sp % B_SCORE_SLOTS, part, carry)
        outs = [acc[:B_V] / acc[B_V:B_V + 1] for _, acc in carry]
        o = jnp.concatenate(outs, axis=0).T
        rows = queries_of(qi)
        y_ref[0, rows, :] = (o * _silu(gate_ref[0, rows, :])).astype(BF16)
        return 0

    for sp in range(B_SCORE_AHEAD):
        scores(0, sp, sp)
    lax.fori_loop(0, nq, qblock, 0, unroll=4)


def _mla_attn(q, k, vt, gate, qb=256, tk=256, span=2):
    b, s, _ = q.shape
    assert (s // (span * tk)) % B_SCORE_SLOTS == 0 and B_SCORE_AHEAD < B_SCORE_SLOTS
    rows = lambda bb, hp: (bb, 0, hp)
    return pl.pallas_call(
        functools.partial(_mla_attn_kernel, qb=qb, tk=tk, span=span),
        grid=(b, B_HEADS // 2),
        in_specs=[
            pl.BlockSpec((1, s, 2 * B_QK_PAD), rows),
            pl.BlockSpec((1, s, 2 * B_QK_PAD), rows),
            pl.BlockSpec((1, 2 * B_V, s), lambda bb, hp: (bb, hp, 0)),
            pl.BlockSpec((1, s, 2 * B_V), rows),
        ],
        out_specs=pl.BlockSpec((1, s, 2 * B_V), rows),
        out_shape=jax.ShapeDtypeStruct((b, s, B_HEADS * B_V), BF16),
        scratch_shapes=[pltpu.VMEM((B_SCORE_SLOTS, 2, span * tk, qb), F32)],
        compiler_params=_params("parallel", "arbitrary"),
        name="b_attn",
    )(q, k, vt, gate)


def _out_proj_kernel(x_ref, y_ref, w_ref, fg_ref, o_ref, *, final_norm):
    xn = x_ref[...] + jnp.dot(y_ref[...], w_ref[...], preferred_element_type=F32)
    if final_norm:
        xn = _rms(xn, fg_ref[...])
    o_ref[...] = xn


def _out_proj(x2, y2, w_bf, fg, final_norm, tm=1024):
    t = x2.shape[0]
    tok = pl.BlockSpec((tm, D_MODEL), lambda i: (i, 0))
    return pl.pallas_call(
        functools.partial(_out_proj_kernel, final_norm=final_norm),
        grid=(t // tm,),
        in_specs=[tok, tok,
                  pl.BlockSpec((D_MODEL, D_MODEL), lambda i: (0, 0)),
                  pl.BlockSpec((1, D_MODEL), lambda i: (0, 0))],
        out_specs=tok,
        out_shape=jax.ShapeDtypeStruct((t, D_MODEL), F32),
        compiler_params=_params("parallel"),
        name="out_proj",
    )(x2, y2, w_bf, fg)


def _mla_layer(x, o_prev, gate_prev, w_out_prev, g, wts, coeffs, fg, final_norm):
    b, s, _ = x.shape
    w1, qg, wq, kvg, wk, wvt, w_out = wts
    x, q, k, vt, gate = _b_proj(x, o_prev, gate_prev, w_out_prev, g, w1, qg, wq, kvg, wk, wvt,
                                coeffs)
    y = _mla_attn(q, k, vt, gate)
    out = _out_proj(x.reshape(b * s, D_MODEL), y.reshape(b * s, D_MODEL), w_out, fg, final_norm)
    return out.reshape(b, s, D_MODEL)


def _prep_mla_weights(w_in, q_g, w_q_up, kv_g, w_kv_up, w_out):
    c0, c1, c2 = B_Q_LORA, B_Q_LORA + B_KV_LORA, B_Q_LORA + B_KV_LORA + B_ROPE
    pad_r = B_QK_PAD - B_NOPE - B_ROPE
    w_kr = jnp.pad(w_in[:, c1:c2], ((0, 0), (B_NOPE, pad_r)))
    w1 = jnp.concatenate([w_in[:, :c1], w_kr, w_in[:, c2:]], axis=1).astype(BF16)
    wq = w_q_up.reshape(B_Q_LORA, B_HEADS, B_NOPE + B_ROPE)
    wq = jnp.pad(wq, ((0, 0), (0, 0), (0, pad_r))).reshape(B_Q_LORA, B_HEADS * B_QK_PAD)
    wkv = w_kv_up.reshape(B_KV_LORA, B_HEADS, B_NOPE + B_V)
    wk = jnp.pad(wkv[:, :, :B_NOPE], ((0, 0), (0, 0), (0, B_QK_PAD - B_NOPE)))
    wk = wk.reshape(B_KV_LORA, B_HEADS * B_QK_PAD)
    wvt = wkv[:, :, B_NOPE:].reshape(B_KV_LORA, B_HEADS * B_V).T
    return (w1, q_g.reshape(1, -1), wq.astype(BF16), kv_g.reshape(1, -1),
            wk.astype(BF16), wvt.astype(BF16), w_out.astype(BF16))


def _trunk(x, norm_g, a_wts, b_wts, coeffs, fg):
    for i in range(DEPTH):
        g = norm_g[i].reshape(1, D_MODEL)
        j = i // 2
        if i % 2 == 0:
            w_in_a, w_out_a = a_wts[j]
            o_a, gate_a = _dilated_layer(x, g, w_in_a)
        else:
            x = _mla_layer(x, o_a, gate_a, w_out_a, g, b_wts[j], coeffs, fg,
                           final_norm=(i == DEPTH - 1))
    return x


def kernel(x_prompt, x_sample, norm_g, a_w_in, a_w_out, b_w_in, b_q_norm_g, b_w_q_up,
           b_kv_norm_g, b_w_kv_up, b_w_out, final_norm_g):
    a_wts = [(a_w_in[j].astype(BF16).reshape(D_MODEL, A_NQKV + 1, A_GROUP_WIDTH).transpose(1, 0, 2),
              a_w_out[j].astype(BF16)) for j in range(a_w_in.shape[0])]
    b_wts = [_prep_mla_weights(b_w_in[j], b_q_norm_g[j], b_w_q_up[j], b_kv_norm_g[j],
                               b_w_kv_up[j], b_w_out[j]) for j in range(b_w_in.shape[0])]
    fg = final_norm_g.reshape(1, D_MODEL)
    outs = []
    for x in (x_prompt, x_sample):
        coeffs = _rope_coeffs(x.shape[1])
        outs.append(_trunk(x, norm_g, a_wts, b_wts, coeffs, fg))
    return tuple(outs)
```

```python
import functools

import numpy as np
import jax
import jax.numpy as jnp
from jax import lax
from jax.experimental import pallas as pl
from jax.experimental.pallas import tpu as pltpu

F32 = jnp.float32
BF16 = jnp.bfloat16

D_MODEL = 1024
DEPTH = 4
RMS_EPS = 1e-6
NEG_BIG = -1e30
LOG2E = float(np.log2(np.e))
LANES = 128

A_GROUPS = ((128, 1), (512, 4), (2048, 16))
A_DILS = tuple(d for _, d in A_GROUPS)
A_HEADS = 8
A_HEAD_DIM = 128
A_GROUP_WIDTH = A_HEADS * A_HEAD_DIM
A_NQKV = 3 * len(A_GROUPS)
ALIBI_MAX_EXP = 8.0
A_RADIUS = 64
A_QSUB = 128
A_KWIN = A_QSUB + 2 * A_RADIUS
A_SCORE_AHEAD = 2
A_MERGE_BEHIND = 2
A_Q_PRESCALE = float(LOG2E * A_HEAD_DIM ** -0.5)

B_HEADS = 16
B_NOPE = 64
B_ROPE = 32
B_V = 64
B_Q_LORA = 384
B_KV_LORA = 256
B_QK_PAD = 128
B_ONES_ROWS = 16
B_SCORE_AHEAD = 1
B_SCORE_SLOTS = 2
ROPE_THETA = 10000.0
B_Q_PRESCALE = float(LOG2E * (B_NOPE + B_ROPE) ** -0.5)

VMEM_LIMIT = 56 * 1024 * 1024


def _params(*sem):
    return pltpu.CompilerParams(dimension_semantics=sem, vmem_limit_bytes=VMEM_LIMIT)


def _rms(xf, g):
    ms = jnp.mean(xf * xf, axis=-1, keepdims=True)
    return xf * lax.rsqrt(ms + RMS_EPS) * g


def _silu(g):
    return g * (1.0 / (1.0 + jnp.exp(-g)))


def _dot_nt(a, b):
    return lax.dot_general(a, b, (((1,), (1,)), ((), ())), preferred_element_type=F32)


def _a_proj_kernel(x_ref, g_ref, w_ref, qkv0_ref, qkv1_ref, qkv2_ref, gate_ref,
                   h_scr, h1_scr, h2_scr, hf_scr, hfm_scr):
    tm = x_ref.shape[1]
    nslab = D_MODEL // LANES
    ratio = A_DILS[2] // A_DILS[1]
    assert A_DILS[1] == ratio
    n_mid = tm // A_DILS[1]
    n_wide = tm // A_DILS[2]

    h = _rms(x_ref[0], g_ref[...])
    h_scr[...] = h.astype(BF16)
    for c in range(nslab):
        hf_scr[c * tm:(c + 1) * tm, :] = h[:, c * LANES:(c + 1) * LANES]

    def permute_mid(r):
        for c in range(nslab):
            piece = hf_scr[pl.ds(c * tm + r, n_mid, stride=A_DILS[1]), :]
            hfm_scr[c * tm + r * n_mid:c * tm + (r + 1) * n_mid, :] = piece
            h1_scr[r * n_mid:(r + 1) * n_mid, c * LANES:(c + 1) * LANES] = piece.astype(BF16)

    def permute_wide(r):
        r_mid, a = r % A_DILS[1], r // A_DILS[1]
        for c in range(nslab):
            piece = hfm_scr[pl.ds(c * tm + r_mid * n_mid + a, n_wide, stride=ratio), :]
            h2_scr[r * n_wide:(r + 1) * n_wide, c * LANES:(c + 1) * LANES] = piece.astype(BF16)

    def project(lhs_scr, m):
        acc = jnp.dot(lhs_scr[...], w_ref[m], preferred_element_type=F32)
        return acc * A_Q_PRESCALE if m % 3 == 0 else acc

    def head_cols(hh):
        return slice(hh * A_HEAD_DIM, (hh + 1) * A_HEAD_DIM)

    for m in range(3):
        acc = project(h_scr, m).astype(BF16)
        for hh in range(A_HEADS):
            qkv0_ref[0, m, hh, 0] = acc[:, head_cols(hh)]
        if m < 2:
            for r in range(m * 2, m * 2 + 2):
                permute_mid(r)
    gate_ref[0] = jnp.dot(h_scr[...], w_ref[A_NQKV], preferred_element_type=F32)
    for m in range(3, 6):
        acc = project(h1_scr, m).astype(BF16)
        for r in range(A_DILS[1]):
            for hh in range(A_HEADS):
                qkv1_ref[0, m - 3, hh, r] = acc[r * n_mid:(r + 1) * n_mid, head_cols(hh)]
        if m == 3:
            for r in range(A_DILS[2]):
                permute_wide(r)
    for m in range(6, 9):
        acc = project(h2_scr, m).astype(BF16)
        for r in range(A_DILS[2]):
            for hh in range(A_HEADS):
                qkv2_ref[0, m - 6, hh, r] = acc[r * n_wide:(r + 1) * n_wide, head_cols(hh)]


def _a_proj(x, g, w_bf, tm=512):
    b, s, _ = x.shape
    gw = A_GROUP_WIDTH
    tok = lambda bb, ii: (bb, ii, 0)
    return pl.pallas_call(
        _a_proj_kernel,
        grid=(b, s // tm),
        in_specs=[
            pl.BlockSpec((1, tm, D_MODEL), tok),
            pl.BlockSpec((1, D_MODEL), lambda bb, ii: (0, 0)),
            pl.BlockSpec((A_NQKV + 1, D_MODEL, gw), lambda bb, ii: (0, 0, 0),
                         pipeline_mode=pl.Buffered(1)),
        ],
        out_specs=[pl.BlockSpec((1, 3, A_HEADS, d, tm // d, A_HEAD_DIM),
                                lambda bb, ii: (bb, 0, 0, 0, ii, 0))
                   for d in A_DILS] + [pl.BlockSpec((1, tm, gw), tok)],
        out_shape=[jax.ShapeDtypeStruct((b, 3, A_HEADS, d, s // d, A_HEAD_DIM), BF16)
                   for d in A_DILS] + [jax.ShapeDtypeStruct((b, s, gw), F32)],
        scratch_shapes=[pltpu.VMEM((tm, D_MODEL), BF16),
                        pltpu.VMEM((tm, D_MODEL), BF16),
                        pltpu.VMEM((tm, D_MODEL), BF16),
                        pltpu.VMEM((D_MODEL // LANES * tm, LANES), F32),
                        pltpu.VMEM((D_MODEL // LANES * tm, LANES), F32)],
        compiler_params=_params("parallel", "parallel"),
        name="a_proj",
    )(x, g, w_bf)


def _a_attn_kernel(q0, k0, v0, q1, k1, v1, q2, k2, v2, bias_ref, out_ref, m_scr, l_scr):
    o_ref = out_ref.at[0]
    seq = o_ref.shape[0]
    ntile = seq // A_QSUB
    ones_cols = jnp.ones((A_KWIN, LANES), BF16)

    def place(group, t):
        dil = A_DILS[group]
        seq_l = seq // dil
        nsub = seq_l // A_QSUB
        r, sub = divmod(t, nsub)
        row0 = sub * A_QSUB
        start = min(max(row0 - A_RADIUS, 0), seq_l - A_KWIN)
        case = 0 if sub == 0 else (2 if sub == nsub - 1 else 1)
        if dil == 1:
            rows = pl.ds(row0, A_QSUB)
        else:
            rows = pl.ds(row0 * dil + r, A_QSUB, stride=dil)
        return r, row0, start, case, rows

    refs = ((q0, k0, v0), (q1, k1, v1), (q2, k2, v2))
    first = len(A_GROUPS) - 1
    items = [(g, place(g, t)) for g in reversed(range(len(A_GROUPS))) for t in range(ntile)]
    scs, stats = {}, {}

    def score(u):
        group, (r, row0, start, case, _) = items[u]
        q_ref, k_ref, _ = refs[group]
        scs[u] = (_dot_nt(q_ref[0, 0, 0, r, pl.ds(row0, A_QSUB), :],
                          k_ref[0, 0, 0, r, pl.ds(start, A_KWIN), :])
                  + bias_ref[0, 3 * group + case])

    def softmax_values(u):
        group, (r, _, start, _, rows) = items[u]
        v_ref = refs[group][2]
        sc = scs.pop(u)
        m = jnp.max(sc, axis=-1, keepdims=True)
        if group == first:
            m_old = None
            m = jnp.broadcast_to(m, (A_QSUB, LANES))
        else:
            m_old = m_scr[rows, :]
            m = jnp.maximum(m_old, m)
        p = jnp.exp2(sc - jnp.concatenate([m, m], axis=1)).astype(BF16)
        v1 = jnp.concatenate([v_ref[0, 0, 0, r, pl.ds(start, A_KWIN), :], ones_cols], axis=1)
        acc = jnp.dot(p, v1, preferred_element_type=F32)
        stats[u] = (m_old, m, acc[:, A_HEAD_DIM:], acc[:, :A_HEAD_DIM])

    def merge(u):
        group, (_, _, _, _, rows) = items[u]
        m_old, m, l, acc = stats.pop(u)
        m_scr[rows, :] = m
        if group == first:
            l_scr[rows, :] = l
            o_ref[rows, :] = acc
        else:
            a_old = jnp.exp2(m_old - m)
            l_scr[rows, :] = l_scr[rows, :] * a_old + l
            o_ref[rows, :] = o_ref[rows, :] * a_old + acc

    n = len(items)
    for s in range(n + A_SCORE_AHEAD + A_MERGE_BEHIND):
        if s < n:
            score(s)
        if 0 <= s - A_SCORE_AHEAD < n:
            softmax_values(s - A_SCORE_AHEAD)
        if 0 <= s - A_SCORE_AHEAD - A_MERGE_BEHIND < n:
            merge(s - A_SCORE_AHEAD - A_MERGE_BEHIND)

    o_ref[...] = o_ref[...] / l_scr[...]


def _alibi_bias_tables():
    n = len(A_GROUPS) * A_HEADS
    slopes = (2.0 ** (-ALIBI_MAX_EXP * np.arange(1, n + 1) / n)).astype(np.float32)
    slopes = slopes.reshape(len(A_GROUPS), A_HEADS)
    i = np.arange(A_QSUB)[:, None]
    j = np.arange(A_KWIN)[None, :]
    out = np.empty((A_HEADS, 3 * len(A_GROUPS), A_QSUB, A_KWIN), np.float32)
    for g, dil in enumerate(A_DILS):
        for case, shift in enumerate((0, -A_RADIUS, -2 * A_RADIUS)):
            delta = np.abs(j - i + shift)
            bias = (-slopes[g][:, None, None] * (delta * dil)[None]).astype(np.float32)
            bias = bias * np.float32(LOG2E)
            out[:, 3 * g + case] = np.where((delta <= A_RADIUS)[None], bias, np.float32(NEG_BIG))
    return out


def _a_attn(qkvs, seq):
    b = qkvs[0].shape[0]
    in_specs, args = [], []
    for qkv, dil in zip(qkvs, A_DILS):
        for n in range(3):
            in_specs.append(pl.BlockSpec((1, 1, 1, dil, seq // dil, A_HEAD_DIM),
                                         lambda bb, hh, n=n: (bb, n, hh, 0, 0, 0)))
            args.append(qkv)
    in_specs.append(pl.BlockSpec((1, 3 * len(A_GROUPS), A_QSUB, A_KWIN),
                                 lambda bb, hh: (hh, 0, 0, 0)))
    args.append(jnp.asarray(_alibi_bias_tables()))
    return pl.pallas_call(
        _a_attn_kernel,
        grid=(b, A_HEADS),
        in_specs=in_specs,
        out_specs=pl.BlockSpec((1, seq, A_HEAD_DIM), lambda bb, hh: (bb, 0, hh)),
        out_shape=jax.ShapeDtypeStruct((b, seq, A_GROUP_WIDTH), F32),
        scratch_shapes=[pltpu.VMEM((seq, LANES), F32), pltpu.VMEM((seq, LANES), F32)],
        compiler_params=_params("parallel", "arbitrary"),
        name="a_attn",
    )(*args)


def _dilated_layer(x, g, w_in_bf):
    qkv0, qkv1, qkv2, gate = _a_proj(x, g, w_in_bf)
    return _a_attn((qkv0, qkv1, qkv2), x.shape[1]), gate


def _rope(t, ca, cm, cp):
    half = B_ROPE // 2
    return t * ca + pltpu.roll(t, B_QK_PAD - half, 1) * cm + pltpu.roll(t, half, 1) * cp


def _b_proj_kernel(x_ref, o_ref, pgate_ref, wprev_ref, g_ref, w1_ref, qg_ref, wq_ref, kvg_ref,
                   wk_ref, wvt_ref, ca_ref, cm_ref, cp_ref,
                   xnew_ref, q_ref, k_ref, vt_ref, gate_ref):
    y = (o_ref[0] * _silu(pgate_ref[0])).astype(BF16)
    x = x_ref[0] + jnp.dot(y, wprev_ref[...], preferred_element_type=F32)
    xnew_ref[0] = x
    h = _rms(x, g_ref[...]).astype(BF16)
    proj = jnp.dot(h, w1_ref[...], preferred_element_type=F32)
    c_q = proj[:, :B_Q_LORA]
    c_kv = proj[:, B_Q_LORA:B_Q_LORA + B_KV_LORA]
    k_r = proj[:, B_Q_LORA + B_KV_LORA:B_Q_LORA + B_KV_LORA + B_QK_PAD]
    gate_ref[0] = proj[:, B_Q_LORA + B_KV_LORA + B_QK_PAD:]
    ca, cm, cp = ca_ref[...], cm_ref[...], cp_ref[...]

    qn = _rms(c_q, qg_ref[...]).astype(BF16)
    q = jnp.dot(qn, wq_ref[...], preferred_element_type=F32)
    for hh in range(B_HEADS):
        cs = slice(hh * B_QK_PAD, (hh + 1) * B_QK_PAD)
        q_ref[0, :, cs] = (_rope(q[:, cs], ca, cm, cp) * B_Q_PRESCALE).astype(BF16)

    kvn = _rms(c_kv, kvg_ref[...]).astype(BF16)
    kn = jnp.dot(kvn, wk_ref[...], preferred_element_type=F32)
    k_rot = _rope(k_r, ca, cm, cp)
    for hh in range(B_HEADS):
        cs = slice(hh * B_QK_PAD, (hh + 1) * B_QK_PAD)
        k_ref[0, :, cs] = (kn[:, cs] + k_rot).astype(BF16)
    vt_ref[0] = _dot_nt(wvt_ref[...], kvn).astype(BF16)


def _rope_coeffs(s):
    half = B_ROPE // 2
    inv_freq = (ROPE_THETA ** (-np.arange(0, B_ROPE, 2) / B_ROPE)).astype(np.float32)
    ang = jnp.arange(s, dtype=F32)[:, None] * jnp.asarray(inv_freq)[None, :]
    cos, sin = jnp.cos(ang), jnp.sin(ang)
    ones = jnp.ones((s, B_NOPE), F32)
    zn = jnp.zeros((s, B_NOPE), F32)
    zh = jnp.zeros((s, half), F32)
    zt = jnp.zeros((s, B_QK_PAD - B_NOPE - B_ROPE), F32)
    ca = jnp.concatenate([ones, cos, cos, zt], axis=1)
    cm = jnp.concatenate([zn, -sin, zh, zt], axis=1)
    cp = jnp.concatenate([zn, zh, sin, zt], axis=1)
    return ca, cm, cp


def _b_proj(x, o_prev, gate_prev, w_out_prev, g, w1, qg, wq, kvg, wk, wvt, coeffs, tm=512):
    b, s, _ = x.shape
    n1 = w1.shape[1]
    const = lambda bb, ii: (0, 0)
    tok = lambda bb, ii: (bb, ii, 0)
    qk_w = B_HEADS * B_QK_PAD
    v_w = B_HEADS * B_V
    return pl.pallas_call(
        _b_proj_kernel,
        grid=(b, s // tm),
        in_specs=[
            pl.BlockSpec((1, tm, D_MODEL), tok),
            pl.BlockSpec((1, tm, A_GROUP_WIDTH), tok),
            pl.BlockSpec((1, tm, A_GROUP_WIDTH), tok),
            pl.BlockSpec((A_GROUP_WIDTH, D_MODEL), const),
            pl.BlockSpec((1, D_MODEL), const),
            pl.BlockSpec((D_MODEL, n1), const),
            pl.BlockSpec((1, B_Q_LORA), const),
            pl.BlockSpec((B_Q_LORA, qk_w), const),
            pl.BlockSpec((1, B_KV_LORA), const),
            pl.BlockSpec((B_KV_LORA, qk_w), const),
            pl.BlockSpec((v_w, B_KV_LORA), const),
            pl.BlockSpec((tm, B_QK_PAD), lambda bb, ii: (ii, 0)),
            pl.BlockSpec((tm, B_QK_PAD), lambda bb, ii: (ii, 0)),
            pl.BlockSpec((tm, B_QK_PAD), lambda bb, ii: (ii, 0)),
        ],
        out_specs=[
            pl.BlockSpec((1, tm, D_MODEL), tok),
            pl.BlockSpec((1, tm, qk_w), tok),
            pl.BlockSpec((1, tm, qk_w), tok),
            pl.BlockSpec((1, v_w, tm), lambda bb, ii: (bb, 0, ii)),
            pl.BlockSpec((1, tm, v_w), tok),
        ],
        out_shape=[
            jax.ShapeDtypeStruct((b, s, D_MODEL), F32),
            jax.ShapeDtypeStruct((b, s, qk_w), BF16),
            jax.ShapeDtypeStruct((b, s, qk_w), BF16),
            jax.ShapeDtypeStruct((b, v_w, s), BF16),
            jax.ShapeDtypeStruct((b, s, v_w), F32),
        ],
        compiler_params=_params("parallel", "parallel"),
        name="b_proj",
    )(x, o_prev, gate_prev, w_out_prev, g, w1, qg, wq, kvg, wk, wvt, *coeffs)


def _mla_attn_kernel(q_ref, k_ref, vt_ref, gate_ref, y_ref, st_scr, *, qb, tk, span):
    s = k_ref.shape[1]
    nq = s // qb
    nspan = s // (span * tk)

    def keys_of(c):
        return pl.ds(pl.multiple_of(c * tk, tk), tk)

    def queries_of(qi):
        return pl.ds(pl.multiple_of(qi * qb, qb), qb)

    def scores(qi, sp, slot):
        keys = pl.ds(pl.multiple_of(sp * (span * tk), span * tk), span * tk)
        for hd in range(2):
            cs = slice(hd * B_QK_PAD, (hd + 1) * B_QK_PAD)
            st_scr[slot, hd] = _dot_nt(k_ref[0, keys, cs], q_ref[0, queries_of(qi), cs])

    ones_rows = jnp.ones((B_ONES_ROWS, tk), BF16)

    def update(c, slot, part, carry):
        new = []
        for hd in range(2):
            m, acc = carry[hd]
            vrows = slice(hd * B_V, (hd + 1) * B_V)
            st = st_scr[slot, hd, part * tk:(part + 1) * tk, :]
            mn = jnp.maximum(m, jnp.max(st, axis=0, keepdims=True))
            alpha = jnp.exp2(m - mn)
            p = jnp.exp2(st - mn).astype(BF16)
            v1 = jnp.concatenate([vt_ref[0, vrows, keys_of(c)], ones_rows], axis=0)
            acc = acc * alpha + jnp.dot(v1, p, preferred_element_type=F32)
            new.append((mn, acc))
        return tuple(new)

    def qblock(qi, _):
        init = (jnp.full((1, qb), NEG_BIG, F32), jnp.zeros((B_V + B_ONES_ROWS, qb), F32))
        carry = (init, init)
        for sp in range(nspan):
            nxt = sp + B_SCORE_AHEAD
            if nxt < nspan:
                scores(qi, nxt, nxt % B_SCORE_SLOTS)
            else:
                scores(jnp.minimum(qi + 1, nq - 1), nxt - nspan, nxt % B_SCORE_SLOTS)
            for part in range(span):
                carry = update(sp * span + part, sp % B_SCORE_SLOTS, part, carry)
        outs = [acc[:B_V] / acc[B_V:B_V + 1] for _, acc in carry]
        o = jnp.concatenate(outs, axis=0).T
        rows = queries_of(qi)
        y_ref[0, rows, :] = (o * _silu(gate_ref[0, rows, :])).astype(BF16)
        return 0

    for sp in range(B_SCORE_AHEAD):
        scores(0, sp, sp)
    lax.fori_loop(0, nq, qblock, 0, unroll=4)


def _mla_attn(q, k, vt, gate, qb=256, tk=256, span=2):
    b, s, _ = q.shape
    assert (s // (span * tk)) % B_SCORE_SLOTS == 0 and B_SCORE_AHEAD < B_SCORE_SLOTS
    rows = lambda bb, hp: (bb, 0, hp)
    return pl.pallas_call(
        functools.partial(_mla_attn_kernel, qb=qb, tk=tk, span=span),
        grid=(b, B_HEADS // 2),
        in_specs=[
            pl.BlockSpec((1, s, 2 * B_QK_PAD), rows),
            pl.BlockSpec((1, s, 2 * B_QK_PAD), rows),
            pl.BlockSpec((1, 2 * B_V, s), lambda bb, hp: (bb, hp, 0)),
            pl.BlockSpec((1, s, 2 * B_V), rows),
        ],
        out_specs=pl.BlockSpec((1, s, 2 * B_V), rows),
        out_shape=jax.ShapeDtypeStruct((b, s, B_HEADS * B_V), BF16),
        scratch_shapes=[pltpu.VMEM((B_SCORE_SLOTS, 2, span * tk, qb), F32)],
        compiler_params=_params("parallel", "arbitrary"),
        name="b_attn",
    )(q, k, vt, gate)


def _out_proj_kernel(x_ref, y_ref, w_ref, fg_ref, o_ref, *, final_norm):
    xn = x_ref[...] + jnp.dot(y_ref[...], w_ref[...], preferred_element_type=F32)
    if final_norm:
        xn = _rms(xn, fg_ref[...])
    o_ref[...] = xn


def _out_proj(x2, y2, w_bf, fg, final_norm, tm=1024):
    t = x2.shape[0]
    tok = pl.BlockSpec((tm, D_MODEL), lambda i: (i, 0))
    return pl.pallas_call(
        functools.partial(_out_proj_kernel, final_norm=final_norm),
        grid=(t // tm,),
        in_specs=[tok, tok,
                  pl.BlockSpec((D_MODEL, D_MODEL), lambda i: (0, 0)),
                  pl.BlockSpec((1, D_MODEL), lambda i: (0, 0))],
        out_specs=tok,
        out_shape=jax.ShapeDtypeStruct((t, D_MODEL), F32),
        compiler_params=_params("parallel"),
        name="out_proj",
    )(x2, y2, w_bf, fg)


def _mla_layer(x, o_prev, gate_prev, w_out_prev, g, wts, coeffs, fg, final_norm):
    b, s, _ = x.shape
    w1, qg, wq, kvg, wk, wvt, w_out = wts
    x, q, k, vt, gate = _b_proj(x, o_prev, gate_prev, w_out_prev, g, w1, qg, wq, kvg, wk, wvt,
                                coeffs)
    y = _mla_attn(q, k, vt, gate)
    out = _out_proj(x.reshape(b * s, D_MODEL), y.reshape(b * s, D_MODEL), w_out, fg, final_norm)
    return out.reshape(b, s, D_MODEL)


def _prep_mla_weights(w_in, q_g, w_q_up, kv_g, w_kv_up, w_out):
    c0, c1, c2 = B_Q_LORA, B_Q_LORA + B_KV_LORA, B_Q_LORA + B_KV_LORA + B_ROPE
    pad_r = B_QK_PAD - B_NOPE - B_ROPE
    w_kr = jnp.pad(w_in[:, c1:c2], ((0, 0), (B_NOPE, pad_r)))
    w1 = jnp.concatenate([w_in[:, :c1], w_kr, w_in[:, c2:]], axis=1).astype(BF16)
    wq = w_q_up.reshape(B_Q_LORA, B_HEADS, B_NOPE + B_ROPE)
    wq = jnp.pad(wq, ((0, 0), (0, 0), (0, pad_r))).reshape(B_Q_LORA, B_HEADS * B_QK_PAD)
    wkv = w_kv_up.reshape(B_KV_LORA, B_HEADS, B_NOPE + B_V)
    wk = jnp.pad(wkv[:, :, :B_NOPE], ((0, 0), (0, 0), (0, B_QK_PAD - B_NOPE)))
    wk = wk.reshape(B_KV_LORA, B_HEADS * B_QK_PAD)
    wvt = wkv[:, :, B_NOPE:].reshape(B_KV_LORA, B_HEADS * B_V).T
    return (w1, q_g.reshape(1, -1), wq.astype(BF16), kv_g.reshape(1, -1),
            wk.astype(BF16), wvt.astype(BF16), w_out.astype(BF16))


def _trunk(x, norm_g, a_wts, b_wts, coeffs, fg):
    for i in range(DEPTH):
        g = norm_g[i].reshape(1, D_MODEL)
        j = i // 2
        if i % 2 == 0:
            w_in_a, w_out_a = a_wts[j]
            o_a, gate_a = _dilated_layer(x, g, w_in_a)
        else:
            x = _mla_layer(x, o_a, gate_a, w_out_a, g, b_wts[j], coeffs, fg,
                           final_norm=(i == DEPTH - 1))
    return x


def kernel(x_prompt, x_sample, norm_g, a_w_in, a_w_out, b_w_in, b_q_norm_g, b_w_q_up,
           b_kv_norm_g, b_w_kv_up, b_w_out, final_norm_g):
    a_wts = [(a_w_in[j].astype(BF16).reshape(D_MODEL, A_NQKV + 1, A_GROUP_WIDTH).transpose(1, 0, 2),
              a_w_out[j].astype(BF16)) for j in range(a_w_in.shape[0])]
    b_wts = [_prep_mla_weights(b_w_in[j], b_q_norm_g[j], b_w_q_up[j], b_kv_norm_g[j],
                               b_w_kv_up[j], b_w_out[j]) for j in range(b_w_in.shape[0])]
    fg = final_norm_g.reshape(1, D_MODEL)
    outs = []
    for x in (x_prompt, x_sample):
        coeffs = _rope_coeffs(x.shape[1])
        outs.append(_trunk(x, norm_g, a_wts, b_wts, coeffs, fg))
    return tuple(outs)
```

```python
import functools

import numpy as np
import jax
import jax.numpy as jnp
from jax import lax
from jax.experimental import pallas as pl
from jax.experimental.pallas import tpu as pltpu

F32 = jnp.float32
BF16 = jnp.bfloat16

D_MODEL = 1024
DEPTH = 4
RMS_EPS = 1e-6
NEG_BIG = -1e30
LOG2E = float(np.log2(np.e))
LANES = 128

A_GROUPS = ((128, 1), (512, 4), (2048, 16))
A_DILS = tuple(d for _, d in A_GROUPS)
A_HEADS = 8
A_HEAD_DIM = 128
A_GROUP_WIDTH = A_HEADS * A_HEAD_DIM
A_NQKV = 3 * len(A_GROUPS)
ALIBI_MAX_EXP = 8.0
A_RADIUS = 64
A_QSUB = 128
A_KWIN = A_QSUB + 2 * A_RADIUS
A_SCORE_AHEAD = 2
A_MERGE_BEHIND = 2
A_Q_PRESCALE = float(LOG2E * A_HEAD_DIM ** -0.5)

B_HEADS = 16
B_NOPE = 64
B_ROPE = 32
B_V = 64
B_Q_LORA = 384
B_KV_LORA = 256
B_QK_PAD = 128
B_ONES_ROWS = 16
B_SCORE_AHEAD = 1
B_SCORE_SLOTS = 2
ROPE_THETA = 10000.0
B_Q_PRESCALE = float(LOG2E * (B_NOPE + B_ROPE) ** -0.5)

VMEM_LIMIT = 56 * 1024 * 1024


def _params(*sem):
    return pltpu.CompilerParams(dimension_semantics=sem, vmem_limit_bytes=VMEM_LIMIT)


def _rms(xf, g):
    ms = jnp.mean(xf * xf, axis=-1, keepdims=True)
    return xf * lax.rsqrt(ms + RMS_EPS) * g


def _silu(g):
    return g * (1.0 / (1.0 + jnp.exp(-g)))


def _dot_nt(a, b):
    return lax.dot_general(a, b, (((1,), (1,)), ((), ())), preferred_element_type=F32)


def _a_proj_kernel(x_ref, g_ref, w_ref, qkv0_ref, qkv1_ref, qkv2_ref, gate_ref,
                   h_scr, h1_scr, h2_scr, hf_scr, hfm_scr):
    tm = x_ref.shape[1]
    nslab = D_MODEL // LANES
    ratio = A_DILS[2] // A_DILS[1]
    assert A_DILS[1] == ratio
    n_mid = tm // A_DILS[1]
    n_wide = tm // A_DILS[2]

    h = _rms(x_ref[0], g_ref[...])
    h_scr[...] = h.astype(BF16)
    for c in range(nslab):
        hf_scr[c * tm:(c + 1) * tm, :] = h[:, c * LANES:(c + 1) * LANES]

    def permute_mid(r):
        for c in range(nslab):
            piece = hf_scr[pl.ds(c * tm + r, n_mid, stride=A_DILS[1]), :]
            hfm_scr[c * tm + r * n_mid:c * tm + (r + 1) * n_mid, :] = piece
            h1_scr[r * n_mid:(r + 1) * n_mid, c * LANES:(c + 1) * LANES] = piece.astype(BF16)

    def permute_wide(r):
        r_mid, a = r % A_DILS[1], r // A_DILS[1]
        for c in range(nslab):
            piece = hfm_scr[pl.ds(c * tm + r_mid * n_mid + a, n_wide, stride=ratio), :]
            h2_scr[r * n_wide:(r + 1) * n_wide, c * LANES:(c + 1) * LANES] = piece.astype(BF16)

    def project(lhs_scr, m):
        acc = jnp.dot(lhs_scr[...], w_ref[m], preferred_element_type=F32)
        return acc * A_Q_PRESCALE if m % 3 == 0 else acc

    def head_cols(hh):
        return slice(hh * A_HEAD_DIM, (hh + 1) * A_HEAD_DIM)

    for m in range(3):
        acc = project(h_scr, m).astype(BF16)
        for hh in range(A_HEADS):
            qkv0_ref[0, m, hh, 0] = acc[:, head_cols(hh)]
        if m < 2:
            for r in range(m * 2, m * 2 + 2):
                permute_mid(r)
    gate_ref[0] = jnp.dot(h_scr[...], w_ref[A_NQKV], preferred_element_type=F32)
    for m in range(3, 6):
        acc = project(h1_scr, m).astype(BF16)
        for r in range(A_DILS[1]):
            for hh in range(A_HEADS):
                qkv1_ref[0, m - 3, hh, r] = acc[r * n_mid:(r + 1) * n_mid, head_cols(hh)]
        if m == 3:
            for r in range(A_DILS[2]):
                permute_wide(r)
    for m in range(6, 9):
        acc = project(h2_scr, m).astype(BF16)
        for r in range(A_DILS[2]):
            for hh in range(A_HEADS):
                qkv2_ref[0, m - 6, hh, r] = acc[r * n_wide:(r + 1) * n_wide, head_cols(hh)]


def _a_proj(x, g, w_bf, tm=512):
    b, s, _ = x.shape
    gw = A_GROUP_WIDTH
    tok = lambda bb, ii: (bb, ii, 0)
    return pl.pallas_call(
        _a_proj_kernel,
        grid=(b, s // tm),
        in_specs=[
            pl.BlockSpec((1, tm, D_MODEL), tok),
            pl.BlockSpec((1, D_MODEL), lambda bb, ii: (0, 0)),
            pl.BlockSpec((A_NQKV + 1, D_MODEL, gw), lambda bb, ii: (0, 0, 0),
                         pipeline_mode=pl.Buffered(1)),
        ],
        out_specs=[pl.BlockSpec((1, 3, A_HEADS, d, tm // d, A_HEAD_DIM),
                                lambda bb, ii: (bb, 0, 0, 0, ii, 0))
                   for d in A_DILS] + [pl.BlockSpec((1, tm, gw), tok)],
        out_shape=[jax.ShapeDtypeStruct((b, 3, A_HEADS, d, s // d, A_HEAD_DIM), BF16)
                   for d in A_DILS] + [jax.ShapeDtypeStruct((b, s, gw), F32)],
        scratch_shapes=[pltpu.VMEM((tm, D_MODEL), BF16),
                        pltpu.VMEM((tm, D_MODEL), BF16),
                        pltpu.VMEM((tm, D_MODEL), BF16),
                        pltpu.VMEM((D_MODEL // LANES * tm, LANES), F32),
                        pltpu.VMEM((D_MODEL // LANES * tm, LANES), F32)],
        compiler_params=_params("parallel", "parallel"),
        name="a_proj",
    )(x, g, w_bf)


def _a_attn_kernel(q0, k0, v0, q1, k1, v1, q2, k2, v2, bias_ref, out_ref, m_scr, l_scr):
    o_ref = out_ref.at[0]
    seq = o_ref.shape[0]
    ntile = seq // A_QSUB
    ones_cols = jnp.ones((A_KWIN, LANES), BF16)

    def place(group, t):
        dil = A_DILS[group]
        seq_l = seq // dil
        nsub = seq_l // A_QSUB
        r, sub = divmod(t, nsub)
        row0 = sub * A_QSUB
        start = min(max(row0 - A_RADIUS, 0), seq_l - A_KWIN)
        case = 0 if sub == 0 else (2 if sub == nsub - 1 else 1)
        if dil == 1:
            rows = pl.ds(row0, A_QSUB)
        else:
            rows = pl.ds(row0 * dil + r, A_QSUB, stride=dil)
        return r, row0, start, case, rows

    refs = ((q0, k0, v0), (q1, k1, v1), (q2, k2, v2))
    first = len(A_GROUPS) - 1
    items = [(g, place(g, t)) for g in reversed(range(len(A_GROUPS))) for t in range(ntile)]

    def positions(group, t):
        dil = A_DILS[group]
        r, sub = divmod(t, seq // dil // A_QSUB)
        return set(range(sub * A_QSUB * dil + r, (sub + 1) * A_QSUB * dil + r, dil))

    tiles = [(g, t) for g in reversed(range(len(A_GROUPS))) for t in range(ntile)]
    for u in range(len(tiles)):
        for w in range(u + 1, min(u + A_MERGE_BEHIND, len(tiles) - 1) + 1):
            assert not positions(*tiles[u]) & positions(*tiles[w]), (tiles[u], tiles[w])
    scs, stats = {}, {}

    def score(u):
        group, (r, row0, start, case, _) = items[u]
        q_ref, k_ref, _ = refs[group]
        scs[u] = (_dot_nt(q_ref[0, 0, 0, r, pl.ds(row0, A_QSUB), :],
                          k_ref[0, 0, 0, r, pl.ds(start, A_KWIN), :])
                  + bias_ref[0, 3 * group + case])

    def softmax_values(u):
        group, (r, _, start, _, rows) = items[u]
        v_ref = refs[group][2]
        sc = scs.pop(u)
        m = jnp.max(sc, axis=-1, keepdims=True)
        if group == first:
            m_old = None
            m = jnp.broadcast_to(m, (A_QSUB, LANES))
        else:
            m_old = m_scr[rows, :]
            m = jnp.maximum(m_old, m)
        p = jnp.exp2(sc - jnp.concatenate([m, m], axis=1)).astype(BF16)
        v1 = jnp.concatenate([v_ref[0, 0, 0, r, pl.ds(start, A_KWIN), :], ones_cols], axis=1)
        acc = jnp.dot(p, v1, preferred_element_type=F32)
        stats[u] = (m_old, m, acc[:, A_HEAD_DIM:], acc[:, :A_HEAD_DIM])

    def merge(u):
        group, (_, _, _, _, rows) = items[u]
        m_old, m, l, acc = stats.pop(u)
        m_scr[rows, :] = m
        if group == first:
            l_scr[rows, :] = l
            o_ref[rows, :] = acc
        else:
            a_old = jnp.exp2(m_old - m)
            l_scr[rows, :] = l_scr[rows, :] * a_old + l
            o_ref[rows, :] = o_ref[rows, :] * a_old + acc

    n = len(items)
    for s in range(n + A_SCORE_AHEAD + A_MERGE_BEHIND):
        if s < n:
            score(s)
        if 0 <= s - A_SCORE_AHEAD < n:
            softmax_values(s - A_SCORE_AHEAD)
        if 0 <= s - A_SCORE_AHEAD - A_MERGE_BEHIND < n:
            merge(s - A_SCORE_AHEAD - A_MERGE_BEHIND)

    o_ref[...] = o_ref[...] / l_scr[...]


def _alibi_bias_tables():
    n = len(A_GROUPS) * A_HEADS
    slopes = (2.0 ** (-ALIBI_MAX_EXP * np.arange(1, n + 1) / n)).astype(np.float32)
    slopes = slopes.reshape(len(A_GROUPS), A_HEADS)
    i = np.arange(A_QSUB)[:, None]
    j = np.arange(A_KWIN)[None, :]
    out = np.empty((A_HEADS, 3 * len(A_GROUPS), A_QSUB, A_KWIN), np.float32)
    for g, dil in enumerate(A_DILS):
        for case, shift in enumerate((0, -A_RADIUS, -2 * A_RADIUS)):
            delta = np.abs(j - i + shift)
            bias = (-slopes[g][:, None, None] * (delta * dil)[None]).astype(np.float32)
            bias = bias * np.float32(LOG2E)
            out[:, 3 * g + case] = np.where((delta <= A_RADIUS)[None], bias, np.float32(NEG_BIG))
    return out


def _a_attn(qkvs, seq):
    b = qkvs[0].shape[0]
    in_specs, args = [], []
    for qkv, dil in zip(qkvs, A_DILS):
        for n in range(3):
            in_specs.append(pl.BlockSpec((1, 1, 1, dil, seq // dil, A_HEAD_DIM),
                                         lambda bb, hh, n=n: (bb, n, hh, 0, 0, 0)))
            args.append(qkv)
    in_specs.append(pl.BlockSpec((1, 3 * len(A_GROUPS), A_QSUB, A_KWIN),
                                 lambda bb, hh: (hh, 0, 0, 0)))
    args.append(jnp.asarray(_alibi_bias_tables()))
    return pl.pallas_call(
        _a_attn_kernel,
        grid=(b, A_HEADS),
        in_specs=in_specs,
        out_specs=pl.BlockSpec((1, seq, A_HEAD_DIM), lambda bb, hh: (bb, 0, hh)),
        out_shape=jax.ShapeDtypeStruct((b, seq, A_GROUP_WIDTH), F32),
        scratch_shapes=[pltpu.VMEM((seq, LANES), F32), pltpu.VMEM((seq, LANES), F32)],
        compiler_params=_params("parallel", "arbitrary"),
        name="a_attn",
    )(*args)


def _dilated_layer(x, g, w_in_bf):
    qkv0, qkv1, qkv2, gate = _a_proj(x, g, w_in_bf)
    return _a_attn((qkv0, qkv1, qkv2), x.shape[1]), gate


def _rope(t, ca, cm, cp):
    half = B_ROPE // 2
    return t * ca + pltpu.roll(t, B_QK_PAD - half, 1) * cm + pltpu.roll(t, half, 1) * cp


def _b_proj_kernel(x_ref, o_ref, pgate_ref, wprev_ref, g_ref, w1_ref, qg_ref, wq_ref, kvg_ref,
                   wk_ref, wvt_ref, ca_ref, cm_ref, cp_ref,
                   xnew_ref, q_ref, k_ref, vt_ref, gate_ref):
    y = (o_ref[0] * _silu(pgate_ref[0])).astype(BF16)
    x = x_ref[0] + jnp.dot(y, wprev_ref[...], preferred_element_type=F32)
    xnew_ref[0] = x
    h = _rms(x, g_ref[...]).astype(BF16)
    proj = jnp.dot(h, w1_ref[...], preferred_element_type=F32)
    c_q = proj[:, :B_Q_LORA]
    c_kv = proj[:, B_Q_LORA:B_Q_LORA + B_KV_LORA]
    k_r = proj[:, B_Q_LORA + B_KV_LORA:B_Q_LORA + B_KV_LORA + B_QK_PAD]
    gate_ref[0] = proj[:, B_Q_LORA + B_KV_LORA + B_QK_PAD:]
    ca, cm, cp = ca_ref[...], cm_ref[...], cp_ref[...]

    qn = _rms(c_q, qg_ref[...]).astype(BF16)
    q = jnp.dot(qn, wq_ref[...], preferred_element_type=F32)
    for hh in range(B_HEADS):
        cs = slice(hh * B_QK_PAD, (hh + 1) * B_QK_PAD)
        q_ref[0, :, cs] = (_rope(q[:, cs], ca, cm, cp) * B_Q_PRESCALE).astype(BF16)

    kvn = _rms(c_kv, kvg_ref[...]).astype(BF16)
    kn = jnp.dot(kvn, wk_ref[...], preferred_element_type=F32)
    k_rot = _rope(k_r, ca, cm, cp)
    for hh in range(B_HEADS):
        cs = slice(hh * B_QK_PAD, (hh + 1) * B_QK_PAD)
        k_ref[0, :, cs] = (kn[:, cs] + k_rot).astype(BF16)
    vt_ref[0] = _dot_nt(wvt_ref[...], kvn).astype(BF16)


def _rope_coeffs(s):
    half = B_ROPE // 2
    inv_freq = (ROPE_THETA ** (-np.arange(0, B_ROPE, 2) / B_ROPE)).astype(np.float32)
    ang = jnp.arange(s, dtype=F32)[:, None] * jnp.asarray(inv_freq)[None, :]
    cos, sin = jnp.cos(ang), jnp.sin(ang)
    ones = jnp.ones((s, B_NOPE), F32)
    zn = jnp.zeros((s, B_NOPE), F32)
    zh = jnp.zeros((s, half), F32)
    zt = jnp.zeros((s, B_QK_PAD - B_NOPE - B_ROPE), F32)
    ca = jnp.concatenate([ones, cos, cos, zt], axis=1)
    cm = jnp.concatenate([zn, -sin, zh, zt], axis=1)
    cp = jnp.concatenate([zn, zh, sin, zt], axis=1)
    return ca, cm, cp


def _b_proj(x, o_prev, gate_prev, w_out_prev, g, w1, qg, wq, kvg, wk, wvt, coeffs, tm=512):
    b, s, _ = x.shape
    n1 = w1.shape[1]
    const = lambda bb, ii: (0, 0)
    tok = lambda bb, ii: (bb, ii, 0)
    qk_w = B_HEADS * B_QK_PAD
    v_w = B_HEADS * B_V
    return pl.pallas_call(
        _b_proj_kernel,
        grid=(b, s // tm),
        in_specs=[
            pl.BlockSpec((1, tm, D_MODEL), tok),
            pl.BlockSpec((1, tm, A_GROUP_WIDTH), tok),
            pl.BlockSpec((1, tm, A_GROUP_WIDTH), tok),
            pl.BlockSpec((A_GROUP_WIDTH, D_MODEL), const),
            pl.BlockSpec((1, D_MODEL), const),
            pl.BlockSpec((D_MODEL, n1), const),
            pl.BlockSpec((1, B_Q_LORA), const),
            pl.BlockSpec((B_Q_LORA, qk_w), const),
            pl.BlockSpec((1, B_KV_LORA), const),
            pl.BlockSpec((B_KV_LORA, qk_w), const),
            pl.BlockSpec((v_w, B_KV_LORA), const),
            pl.BlockSpec((tm, B_QK_PAD), lambda bb, ii: (ii, 0)),
            pl.BlockSpec((tm, B_QK_PAD), lambda bb, ii: (ii, 0)),
            pl.BlockSpec((tm, B_QK_PAD), lambda bb, ii: (ii, 0)),
        ],
        out_specs=[
            pl.BlockSpec((1, tm, D_MODEL), tok),
            pl.BlockSpec((1, tm, qk_w), tok),
            pl.BlockSpec((1, tm, qk_w), tok),
            pl.BlockSpec((1, v_w, tm), lambda bb, ii: (bb, 0, ii)),
            pl.BlockSpec((1, tm, v_w), tok),
        ],
        out_shape=[
            jax.ShapeDtypeStruct((b, s, D_MODEL), F32),
            jax.ShapeDtypeStruct((b, s, qk_w), BF16),
            jax.ShapeDtypeStruct((b, s, qk_w), BF16),
            jax.ShapeDtypeStruct((b, v_w, s), BF16),
            jax.ShapeDtypeStruct((b, s, v_w), F32),
        ],
        compiler_params=_params("parallel", "parallel"),
        name="b_proj",
    )(x, o_prev, gate_prev, w_out_prev, g, w1, qg, wq, kvg, wk, wvt, *coeffs)


def _mla_attn_kernel(q_ref, k_ref, vt_ref, gate_ref, y_ref, st_scr, *, qb, tk, span):
    s = k_ref.shape[1]
    nq = s // qb
    nspan = s // (span * tk)

    def keys_of(c):
        return pl.ds(pl.multiple_of(c * tk, tk), tk)

    def queries_of(qi):
        return pl.ds(pl.multiple_of(qi * qb, qb), qb)

    def scores(qi, sp, slot):
        keys = pl.ds(pl.multiple_of(sp * (span * tk), span * tk), span * tk)
        for hd in range(2):
            cs = slice(hd * B_QK_PAD, (hd + 1) * B_QK_PAD)
            st_scr[slot, hd] = _dot_nt(k_ref[0, keys, cs], q_ref[0, queries_of(qi), cs])

    ones_rows = jnp.ones((B_ONES_ROWS, tk), BF16)

    def update(c, slot, part, carry):
        new = []
        for hd in range(2):
            m, acc = carry[hd]
            vrows = slice(hd * B_V, (hd + 1) * B_V)
            st = st_scr[slot, hd, part * tk:(part + 1) * tk, :]
            mn = jnp.maximum(m, jnp.max(st, axis=0, keepdims=True))
            alpha = jnp.exp2(m - mn)
            p = jnp.exp2(st - mn).astype(BF16)
            v1 = jnp.concatenate([vt_ref[0, vrows, keys_of(c)], ones_rows], axis=0)
            acc = acc * alpha + jnp.dot(v1, p, preferred_element_type=F32)
            new.append((mn, acc))
        return tuple(new)

    def qblock(qi, _):
        init = (jnp.full((1, qb), NEG_BIG, F32), jnp.zeros((B_V + B_ONES_ROWS, qb), F32))
        carry = (init, init)
        for sp in range(nspan):
            nxt = sp + B_SCORE_AHEAD
            if nxt < nspan:
                scores(qi, nxt, nxt % B_SCORE_SLOTS)
            else:
                scores(jnp.minimum(qi + 1, nq - 1), nxt - nspan, nxt % B_SCORE_SLOTS)
            for part in range(span):
                carry = update(sp * span + part, sp % B_SCORE_SLOTS, part, carry)
        outs = [acc[:B_V] / acc[B_V:B_V + 1] for _, acc in carry]
        o = jnp.concatenate(outs, axis=0).T
        rows = queries_of(qi)
        y_ref[0, rows, :] = (o * _silu(gate_ref[0, rows, :])).astype(BF16)
        return 0

    for sp in range(B_SCORE_AHEAD):
        scores(0, sp, sp)
    lax.fori_loop(0, nq, qblock, 0, unroll=4)


def _mla_attn(q, k, vt, gate, qb=256, tk=256, span=2):
    b, s, _ = q.shape
    assert (s // (span * tk)) % B_SCORE_SLOTS == 0 and B_SCORE_AHEAD < B_SCORE_SLOTS
    rows = lambda bb, hp: (bb, 0, hp)
    return pl.pallas_call(
        functools.partial(_mla_attn_kernel, qb=qb, tk=tk, span=span),
        grid=(b, B_HEADS // 2),
        in_specs=[
            pl.BlockSpec((1, s, 2 * B_QK_PAD), rows),
            pl.BlockSpec((1, s, 2 * B_QK_PAD), rows),
            pl.BlockSpec((1, 2 * B_V, s), lambda bb, hp: (bb, hp, 0)),
            pl.BlockSpec((1, s, 2 * B_V), rows),
        ],
        out_specs=pl.BlockSpec((1, s, 2 * B_V), rows),
        out_shape=jax.ShapeDtypeStruct((b, s, B_HEADS * B_V), BF16),
        scratch_shapes=[pltpu.VMEM((B_SCORE_SLOTS, 2, span * tk, qb), F32)],
        compiler_params=_params("parallel", "arbitrary"),
        name="b_attn",
    )(q, k, vt, gate)


def _out_proj_kernel(x_ref, y_ref, w_ref, fg_ref, o_ref, *, final_norm):
    xn = x_ref[...] + jnp.dot(y_ref[...], w_ref[...], preferred_element_type=F32)
    if final_norm:
        xn = _rms(xn, fg_ref[...])
    o_ref[...] = xn


def _out_proj(x2, y2, w_bf, fg, final_norm, tm=1024):
    t = x2.shape[0]
    tok = pl.BlockSpec((tm, D_MODEL), lambda i: (i, 0))
    return pl.pallas_call(
        functools.partial(_out_proj_kernel, final_norm=final_norm),
        grid=(t // tm,),
        in_specs=[tok, tok,
                  pl.BlockSpec((D_MODEL, D_MODEL), lambda i: (0, 0)),
                  pl.BlockSpec((1, D_MODEL), lambda i: (0, 0))],
        out_specs=tok,
        out_shape=jax.ShapeDtypeStruct((t, D_MODEL), F32),
        compiler_params=_params("parallel"),
        name="out_proj",
    )(x2, y2, w_bf, fg)


def _mla_layer(x, o_prev, gate_prev, w_out_prev, g, wts, coeffs, fg, final_norm):
    b, s, _ = x.shape
    w1, qg, wq, kvg, wk, wvt, w_out = wts
    x, q, k, vt, gate = _b_proj(x, o_prev, gate_prev, w_out_prev, g, w1, qg, wq, kvg, wk, wvt,
                                coeffs)
    y = _mla_attn(q, k, vt, gate)
    out = _out_proj(x.reshape(b * s, D_MODEL), y.reshape(b * s, D_MODEL), w_out, fg, final_norm)
    return out.reshape(b, s, D_MODEL)


def _prep_mla_weights(w_in, q_g, w_q_up, kv_g, w_kv_up, w_out):
    c0, c1, c2 = B_Q_LORA, B_Q_LORA + B_KV_LORA, B_Q_LORA + B_KV_LORA + B_ROPE
    pad_r = B_QK_PAD - B_NOPE - B_ROPE
    w_kr = jnp.pad(w_in[:, c1:c2], ((0, 0), (B_NOPE, pad_r)))
    w1 = jnp.concatenate([w_in[:, :c1], w_kr, w_in[:, c2:]], axis=1).astype(BF16)
    wq = w_q_up.reshape(B_Q_LORA, B_HEADS, B_NOPE + B_ROPE)
    wq = jnp.pad(wq, ((0, 0), (0, 0), (0, pad_r))).reshape(B_Q_LORA, B_HEADS * B_QK_PAD)
    wkv = w_kv_up.reshape(B_KV_LORA, B_HEADS, B_NOPE + B_V)
    wk = jnp.pad(wkv[:, :, :B_NOPE], ((0, 0), (0, 0), (0, B_QK_PAD - B_NOPE)))
    wk = wk.reshape(B_KV_LORA, B_HEADS * B_QK_PAD)
    wvt = wkv[:, :, B_NOPE:].reshape(B_KV_LORA, B_HEADS * B_V).T
    return (w1, q_g.reshape(1, -1), wq.astype(BF16), kv_g.reshape(1, -1),
            wk.astype(BF16), wvt.astype(BF16), w_out.astype(BF16))


def _trunk(x, norm_g, a_wts, b_wts, coeffs, fg):
    for i in range(DEPTH):
        g = norm_g[i].reshape(1, D_MODEL)
        j = i // 2
        if i % 2 == 0:
            w_in_a, w_out_a = a_wts[j]
            o_a, gate_a = _dilated_layer(x, g, w_in_a)
        else:
            x = _mla_layer(x, o_a, gate_a, w_out_a, g, b_wts[j], coeffs, fg,
                           final_norm=(i == DEPTH - 1))
    return x


def kernel(x_prompt, x_sample, norm_g, a_w_in, a_w_out, b_w_in, b_q_norm_g, b_w_q_up,
           b_kv_norm_g, b_w_kv_up, b_w_out, final_norm_g):
    a_wts = [(a_w_in[j].astype(BF16).reshape(D_MODEL, A_NQKV + 1, A_GROUP_WIDTH).transpose(1, 0, 2),
              a_w_out[j].astype(BF16)) for j in range(a_w_in.shape[0])]
    b_wts = [_prep_mla_weights(b_w_in[j], b_q_norm_g[j], b_w_q_up[j], b_kv_norm_g[j],
                               b_w_kv_up[j], b_w_out[j]) for j in range(b_w_in.shape[0])]
    fg = final_norm_g.reshape(1, D_MODEL)
    outs = []
    for x in (x_prompt, x_sample):
        coeffs = _rope_coeffs(x.shape[1])
        outs.append(_trunk(x, norm_g, a_wts, b_wts, coeffs, fg))
    return tuple(outs)
```

```python
import functools

import numpy as np
import jax
import jax.numpy as jnp
from jax import lax
from jax.experimental import pallas as pl
from jax.experimental.pallas import tpu as pltpu

F32 = jnp.float32
BF16 = jnp.bfloat16

D_MODEL = 1024
DEPTH = 4
RMS_EPS = 1e-6
NEG_BIG = -1e30
LOG2E = float(np.log2(np.e))
LANES = 128

A_GROUPS = ((128, 1), (512, 4), (2048, 16))
A_DILS = tuple(d for _, d in A_GROUPS)
A_HEADS = 8
A_HEAD_DIM = 128
A_GROUP_WIDTH = A_HEADS * A_HEAD_DIM
A_NQKV = 3 * len(A_GROUPS)
ALIBI_MAX_EXP = 8.0
A_RADIUS = 64
A_QSUB = 128
A_KWIN = A_QSUB + 2 * A_RADIUS
A_SCORE_AHEAD = 2
A_MERGE_BEHIND = 2
A_Q_PRESCALE = float(LOG2E * A_HEAD_DIM ** -0.5)

B_HEADS = 16
B_NOPE = 64
B_ROPE = 32
B_V = 64
B_Q_LORA = 384
B_KV_LORA = 256
B_QK_PAD = 128
B_ONES_ROWS = 16
B_SCORE_AHEAD = 1
B_SCORE_SLOTS = 2
ROPE_THETA = 10000.0
B_Q_PRESCALE = float(LOG2E * (B_NOPE + B_ROPE) ** -0.5)

VMEM_LIMIT = 56 * 1024 * 1024


def _params(*sem):
    return pltpu.CompilerParams(dimension_semantics=sem, vmem_limit_bytes=VMEM_LIMIT)


def _rms(xf, g):
    ms = jnp.mean(xf * xf, axis=-1, keepdims=True)
    return xf * lax.rsqrt(ms + RMS_EPS) * g


def _silu(g):
    return g * (1.0 / (1.0 + jnp.exp(-g)))


def _dot_nt(a, b):
    return lax.dot_general(a, b, (((1,), (1,)), ((), ())), preferred_element_type=F32)


def _a_proj_kernel(x_ref, g_ref, w_ref, qkv0_ref, qkv1_ref, qkv2_ref, gate_ref,
                   h_scr, h1_scr, h2_scr, hf_scr, hfm_scr):
    tm = x_ref.shape[1]
    nslab = D_MODEL // LANES
    ratio = A_DILS[2] // A_DILS[1]
    assert A_DILS[1] == ratio
    n_mid = tm // A_DILS[1]
    n_wide = tm // A_DILS[2]

    h = _rms(x_ref[0], g_ref[...])
    h_scr[...] = h.astype(BF16)
    for c in range(nslab):
        hf_scr[c * tm:(c + 1) * tm, :] = h[:, c * LANES:(c + 1) * LANES]

    def permute_mid(r):
        for c in range(nslab):
            piece = hf_scr[pl.ds(c * tm + r, n_mid, stride=A_DILS[1]), :]
            hfm_scr[c * tm + r * n_mid:c * tm + (r + 1) * n_mid, :] = piece
            h1_scr[r * n_mid:(r + 1) * n_mid, c * LANES:(c + 1) * LANES] = piece.astype(BF16)

    def permute_wide(r):
        r_mid, a = r % A_DILS[1], r // A_DILS[1]
        for c in range(nslab):
            piece = hfm_scr[pl.ds(c * tm + r_mid * n_mid + a, n_wide, stride=ratio), :]
            h2_scr[r * n_wide:(r + 1) * n_wide, c * LANES:(c + 1) * LANES] = piece.astype(BF16)

    def project(lhs_scr, m):
        acc = jnp.dot(lhs_scr[...], w_ref[m], preferred_element_type=F32)
        return acc * A_Q_PRESCALE if m % 3 == 0 else acc

    def head_cols(hh):
        return slice(hh * A_HEAD_DIM, (hh + 1) * A_HEAD_DIM)

    for m in range(3):
        acc = project(h_scr, m).astype(BF16)
        for hh in range(A_HEADS):
            qkv0_ref[0, m, hh, 0] = acc[:, head_cols(hh)]
        if m < 2:
            for r in range(m * 2, m * 2 + 2):
                permute_mid(r)
    gate_ref[0] = jnp.dot(h_scr[...], w_ref[A_NQKV], preferred_element_type=F32)
    for m in range(3, 6):
        acc = project(h1_scr, m).astype(BF16)
        for r in range(A_DILS[1]):
            for hh in range(A_HEADS):
                qkv1_ref[0, m - 3, hh, r] = acc[r * n_mid:(r + 1) * n_mid, head_cols(hh)]
        if m == 3:
            for r in range(A_DILS[2]):
                permute_wide(r)
    for m in range(6, 9):
        acc = project(h2_scr, m).astype(BF16)
        for r in range(A_DILS[2]):
            for hh in range(A_HEADS):
                qkv2_ref[0, m - 6, hh, r] = acc[r * n_wide:(r + 1) * n_wide, head_cols(hh)]


def _a_proj(x, g, w_bf, tm=512):
    b, s, _ = x.shape
    gw = A_GROUP_WIDTH
    tok = lambda bb, ii: (bb, ii, 0)
    return pl.pallas_call(
        _a_proj_kernel,
        grid=(b, s // tm),
        in_specs=[
            pl.BlockSpec((1, tm, D_MODEL), tok),
            pl.BlockSpec((1, D_MODEL), lambda bb, ii: (0, 0)),
            pl.BlockSpec((A_NQKV + 1, D_MODEL, gw), lambda bb, ii: (0, 0, 0),
                         pipeline_mode=pl.Buffered(1)),
        ],
        out_specs=[pl.BlockSpec((1, 3, A_HEADS, d, tm // d, A_HEAD_DIM),
                                lambda bb, ii: (bb, 0, 0, 0, ii, 0))
                   for d in A_DILS] + [pl.BlockSpec((1, tm, gw), tok)],
        out_shape=[jax.ShapeDtypeStruct((b, 3, A_HEADS, d, s // d, A_HEAD_DIM), BF16)
                   for d in A_DILS] + [jax.ShapeDtypeStruct((b, s, gw), F32)],
        scratch_shapes=[pltpu.VMEM((tm, D_MODEL), BF16),
                        pltpu.VMEM((tm, D_MODEL), BF16),
                        pltpu.VMEM((tm, D_MODEL), BF16),
                        pltpu.VMEM((D_MODEL // LANES * tm, LANES), F32),
                        pltpu.VMEM((D_MODEL // LANES * tm, LANES), F32)],
        compiler_params=_params("parallel", "parallel"),
        name="a_proj",
    )(x, g, w_bf)


def _a_attn_kernel(q0, k0, v0, q1, k1, v1, q2, k2, v2, bias_ref, out_ref, m_scr, l_scr):
    o_ref = out_ref.at[0]
    seq = o_ref.shape[0]
    ntile = seq // A_QSUB
    ones_cols = jnp.ones((A_KWIN, LANES), BF16)

    def place(group, t):
        dil = A_DILS[group]
        seq_l = seq // dil
        nsub = seq_l // A_QSUB
        r, sub = divmod(t, nsub)
        row0 = sub * A_QSUB
        start = min(max(row0 - A_RADIUS, 0), seq_l - A_KWIN)
        case = 0 if sub == 0 else (2 if sub == nsub - 1 else 1)
        if dil == 1:
            rows = pl.ds(row0, A_QSUB)
        else:
            rows = pl.ds(row0 * dil + r, A_QSUB, stride=dil)
        return r, row0, start, case, rows

    refs = ((q0, k0, v0), (q1, k1, v1), (q2, k2, v2))
    first = len(A_GROUPS) - 1
    items = [(g, place(g, t)) for g in reversed(range(len(A_GROUPS))) for t in range(ntile)]

    def positions(group, t):
        dil = A_DILS[group]
        r, sub = divmod(t, seq // dil // A_QSUB)
        return set(range(sub * A_QSUB * dil + r, (sub + 1) * A_QSUB * dil + r, dil))

    tiles = [(g, t) for g in reversed(range(len(A_GROUPS))) for t in range(ntile)]
    for u in range(len(tiles)):
        for w in range(u + 1, min(u + A_MERGE_BEHIND, len(tiles) - 1) + 1):
            assert not positions(*tiles[u]) & positions(*tiles[w]), (tiles[u], tiles[w])
    scs, stats = {}, {}

    def score(u):
        group, (r, row0, start, case, _) = items[u]
        q_ref, k_ref, _ = refs[group]
        scs[u] = (_dot_nt(q_ref[0, 0, 0, r, pl.ds(row0, A_QSUB), :],
                          k_ref[0, 0, 0, r, pl.ds(start, A_KWIN), :])
                  + bias_ref[0, 3 * group + case])

    def softmax_values(u):
        group, (r, _, start, _, rows) = items[u]
        v_ref = refs[group][2]
        sc = scs.pop(u)
        m = jnp.max(sc, axis=-1, keepdims=True)
        if group == first:
            m_old = None
            m = jnp.broadcast_to(m, (A_QSUB, LANES))
        else:
            m_old = m_scr[rows, :]
            m = jnp.maximum(m_old, m)
        p = jnp.exp2(sc - jnp.concatenate([m, m], axis=1)).astype(BF16)
        v1 = jnp.concatenate([v_ref[0, 0, 0, r, pl.ds(start, A_KWIN), :], ones_cols], axis=1)
        acc = jnp.dot(p, v1, preferred_element_type=F32)
        stats[u] = (m_old, m, acc[:, A_HEAD_DIM:], acc[:, :A_HEAD_DIM])

    def merge(u):
        group, (_, _, _, _, rows) = items[u]
        m_old, m, l, acc = stats.pop(u)
        m_scr[rows, :] = m
        if group == first:
            l_scr[rows, :] = l
            o_ref[rows, :] = acc
        else:
            a_old = jnp.exp2(m_old - m)
            l_scr[rows, :] = l_scr[rows, :] * a_old + l
            o_ref[rows, :] = o_ref[rows, :] * a_old + acc

    n = len(items)
    for s in range(n + A_SCORE_AHEAD + A_MERGE_BEHIND):
        if s < n:
            score(s)
        if 0 <= s - A_SCORE_AHEAD < n:
            softmax_values(s - A_SCORE_AHEAD)
        if 0 <= s - A_SCORE_AHEAD - A_MERGE_BEHIND < n:
            merge(s - A_SCORE_AHEAD - A_MERGE_BEHIND)

    o_ref[...] = o_ref[...] / l_scr[...]


def _alibi_bias_tables():
    n = len(A_GROUPS) * A_HEADS
    slopes = (2.0 ** (-ALIBI_MAX_EXP * np.arange(1, n + 1) / n)).astype(np.float32)
    slopes = slopes.reshape(len(A_GROUPS), A_HEADS)
    i = np.arange(A_QSUB)[:, None]
    j = np.arange(A_KWIN)[None, :]
    out = np.empty((A_HEADS, 3 * len(A_GROUPS), A_QSUB, A_KWIN), np.float32)
    for g, dil in enumerate(A_DILS):
        for case, shift in enumerate((0, -A_RADIUS, -2 * A_RADIUS)):
            delta = np.abs(j - i + shift)
            bias = (-slopes[g][:, None, None] * (delta * dil)[None]).astype(np.float32)
            bias = bias * np.float32(LOG2E)
            out[:, 3 * g + case] = np.where((delta <= A_RADIUS)[None], bias, np.float32(NEG_BIG))
    return out


def _a_attn(qkvs, seq):
    b = qkvs[0].shape[0]
    in_specs, args = [], []
    for qkv, dil in zip(qkvs, A_DILS):
        for n in range(3):
            in_specs.append(pl.BlockSpec((1, 1, 1, dil, seq // dil, A_HEAD_DIM),
                                         lambda bb, hh, n=n: (bb, n, hh, 0, 0, 0)))
            args.append(qkv)
    in_specs.append(pl.BlockSpec((1, 3 * len(A_GROUPS), A_QSUB, A_KWIN),
                                 lambda bb, hh: (hh, 0, 0, 0)))
    args.append(jnp.asarray(_alibi_bias_tables()))
    return pl.pallas_call(
        _a_attn_kernel,
        grid=(b, A_HEADS),
        in_specs=in_specs,
        out_specs=pl.BlockSpec((1, seq, A_HEAD_DIM), lambda bb, hh: (bb, 0, hh)),
        out_shape=jax.ShapeDtypeStruct((b, seq, A_GROUP_WIDTH), F32),
        scratch_shapes=[pltpu.VMEM((seq, LANES), F32), pltpu.VMEM((seq, LANES), F32)],
        compiler_params=_params("parallel", "arbitrary"),
        name="a_attn",
    )(*args)


def _dilated_layer(x, g, w_in_bf):
    qkv0, qkv1, qkv2, gate = _a_proj(x, g, w_in_bf)
    return _a_attn((qkv0, qkv1, qkv2), x.shape[1]), gate


def _rope(t, ca, cm, cp):
    half = B_ROPE // 2
    return t * ca + pltpu.roll(t, B_QK_PAD - half, 1) * cm + pltpu.roll(t, half, 1) * cp


def _b_proj_kernel(x_ref, o_ref, pgate_ref, wprev_ref, g_ref, w1_ref, qg_ref, wq_ref, kvg_ref,
                   wk_ref, wvt_ref, ca_ref, cm_ref, cp_ref,
                   xnew_ref, q_ref, k_ref, vt_ref, gate_ref):
    y = (o_ref[0] * _silu(pgate_ref[0])).astype(BF16)
    x = x_ref[0] + jnp.dot(y, wprev_ref[...], preferred_element_type=F32)
    xnew_ref[0] = x
    h = _rms(x, g_ref[...]).astype(BF16)
    proj = jnp.dot(h, w1_ref[...], preferred_element_type=F32)
    c_q = proj[:, :B_Q_LORA]
    c_kv = proj[:, B_Q_LORA:B_Q_LORA + B_KV_LORA]
    k_r = proj[:, B_Q_LORA + B_KV_LORA:B_Q_LORA + B_KV_LORA + B_QK_PAD]
    gate_ref[0] = proj[:, B_Q_LORA + B_KV_LORA + B_QK_PAD:]
    ca, cm, cp = ca_ref[...], cm_ref[...], cp_ref[...]

    qn = _rms(c_q, qg_ref[...]).astype(BF16)
    q = jnp.dot(qn, wq_ref[...], preferred_element_type=F32)
    for hh in range(B_HEADS):
        cs = slice(hh * B_QK_PAD, (hh + 1) * B_QK_PAD)
        q_ref[0, :, cs] = (_rope(q[:, cs], ca, cm, cp) * B_Q_PRESCALE).astype(BF16)

    kvn = _rms(c_kv, kvg_ref[...]).astype(BF16)
    kn = jnp.dot(kvn, wk_ref[...], preferred_element_type=F32)
    k_rot = _rope(k_r, ca, cm, cp)
    for hh in range(B_HEADS):
        cs = slice(hh * B_QK_PAD, (hh + 1) * B_QK_PAD)
        k_ref[0, :, cs] = (kn[:, cs] + k_rot).astype(BF16)
    vt_ref[0] = _dot_nt(wvt_ref[...], kvn).astype(BF16)


def _rope_coeffs(s):
    half = B_ROPE // 2
    inv_freq = (ROPE_THETA ** (-np.arange(0, B_ROPE, 2) / B_ROPE)).astype(np.float32)
    ang = jnp.arange(s, dtype=F32)[:, None] * jnp.asarray(inv_freq)[None, :]
    cos, sin = jnp.cos(ang), jnp.sin(ang)
    ones = jnp.ones((s, B_NOPE), F32)
    zn = jnp.zeros((s, B_NOPE), F32)
    zh = jnp.zeros((s, half), F32)
    zt = jnp.zeros((s, B_QK_PAD - B_NOPE - B_ROPE), F32)
    ca = jnp.concatenate([ones, cos, cos, zt], axis=1)
    cm = jnp.concatenate([zn, -sin, zh, zt], axis=1)
    cp = jnp.concatenate([zn, zh, sin, zt], axis=1)
    return ca, cm, cp


def _b_proj(x, o_prev, gate_prev, w_out_prev, g, w1, qg, wq, kvg, wk, wvt, coeffs, tm=512):
    b, s, _ = x.shape
    n1 = w1.shape[1]
    const = lambda bb, ii: (0, 0)
    tok = lambda bb, ii: (bb, ii, 0)
    qk_w = B_HEADS * B_QK_PAD
    v_w = B_HEADS * B_V
    return pl.pallas_call(
        _b_proj_kernel,
        grid=(b, s // tm),
        in_specs=[
            pl.BlockSpec((1, tm, D_MODEL), tok),
            pl.BlockSpec((1, tm, A_GROUP_WIDTH), tok),
            pl.BlockSpec((1, tm, A_GROUP_WIDTH), tok),
            pl.BlockSpec((A_GROUP_WIDTH, D_MODEL), const),
            pl.BlockSpec((1, D_MODEL), const),
            pl.BlockSpec((D_MODEL, n1), const),
            pl.BlockSpec((1, B_Q_LORA), const),
            pl.BlockSpec((B_Q_LORA, qk_w), const),
            pl.BlockSpec((1, B_KV_LORA), const),
            pl.BlockSpec((B_KV_LORA, qk_w), const),
            pl.BlockSpec((v_w, B_KV_LORA), const),
            pl.BlockSpec((tm, B_QK_PAD), lambda bb, ii: (ii, 0)),
            pl.BlockSpec((tm, B_QK_PAD), lambda bb, ii: (ii, 0)),
            pl.BlockSpec((tm, B_QK_PAD), lambda bb, ii: (ii, 0)),
        ],
        out_specs=[
            pl.BlockSpec((1, tm, D_MODEL), tok),
            pl.BlockSpec((1, tm, qk_w), tok),
            pl.BlockSpec((1, tm, qk_w), tok),
            pl.BlockSpec((1, v_w, tm), lambda bb, ii: (bb, 0, ii)),
            pl.BlockSpec((1, tm, v_w), tok),
        ],
        out_shape=[
            jax.ShapeDtypeStruct((b, s, D_MODEL), F32),
            jax.ShapeDtypeStruct((b, s, qk_w), BF16),
            jax.ShapeDtypeStruct((b, s, qk_w), BF16),
            jax.ShapeDtypeStruct((b, v_w, s), BF16),
            jax.ShapeDtypeStruct((b, s, v_w), F32),
        ],
        compiler_params=_params("parallel", "parallel"),
        name="b_proj",
    )(x, o_prev, gate_prev, w_out_prev, g, w1, qg, wq, kvg, wk, wvt, *coeffs)


def _mla_attn_kernel(q_ref, k_ref, vt_ref, gate_ref, y_ref, st_scr, *, qb, tk, span):
    s = k_ref.shape[1]
    nq = s // qb
    nspan = s // (span * tk)

    def keys_of(c):
        return pl.ds(pl.multiple_of(c * tk, tk), tk)

    def queries_of(qi):
        return pl.ds(pl.multiple_of(qi * qb, qb), qb)

    def scores(qi, sp, slot):
        keys = pl.ds(pl.multiple_of(sp * (span * tk), span * tk), span * tk)
        for hd in range(2):
            cs = slice(hd * B_QK_PAD, (hd + 1) * B_QK_PAD)
            st_scr[slot, hd] = _dot_nt(k_ref[0, keys, cs], q_ref[0, queries_of(qi), cs])

    ones_rows = jnp.ones((B_ONES_ROWS, tk), BF16)

    def update(c, slot, part, carry):
        new = []
        for hd in range(2):
            m, acc = carry[hd]
            vrows = slice(hd * B_V, (hd + 1) * B_V)
            st = st_scr[slot, hd, part * tk:(part + 1) * tk, :]
            mn = jnp.maximum(m, jnp.max(st, axis=0, keepdims=True))
            alpha = jnp.exp2(m - mn)
            p = jnp.exp2(st - mn).astype(BF16)
            v1 = jnp.concatenate([vt_ref[0, vrows, keys_of(c)], ones_rows], axis=0)
            acc = acc * alpha + jnp.dot(v1, p, preferred_element_type=F32)
            new.append((mn, acc))
        return tuple(new)

    def qblock(qi, _):
        init = (jnp.full((1, qb), NEG_BIG, F32), jnp.zeros((B_V + B_ONES_ROWS, qb), F32))
        carry = (init, init)
        for sp in range(nspan):
            nxt = sp + B_SCORE_AHEAD
            if nxt < nspan:
                scores(qi, nxt, nxt % B_SCORE_SLOTS)
            else:
                scores(jnp.minimum(qi + 1, nq - 1), nxt - nspan, nxt % B_SCORE_SLOTS)
            for part in range(span):
                carry = update(sp * span + part, sp % B_SCORE_SLOTS, part, carry)
        outs = [acc[:B_V] / acc[B_V:B_V + 1] for _, acc in carry]
        o = jnp.concatenate(outs, axis=0).T
        rows = queries_of(qi)
        y_ref[0, rows, :] = (o * _silu(gate_ref[0, rows, :])).astype(BF16)
        return 0

    for sp in range(B_SCORE_AHEAD):
        scores(0, sp, sp)
    lax.fori_loop(0, nq, qblock, 0, unroll=8)


def _mla_attn(q, k, vt, gate, qb=256, tk=256, span=2):
    b, s, _ = q.shape
    assert (s // (span * tk)) % B_SCORE_SLOTS == 0 and B_SCORE_AHEAD < B_SCORE_SLOTS
    rows = lambda bb, hp: (bb, 0, hp)
    return pl.pallas_call(
        functools.partial(_mla_attn_kernel, qb=qb, tk=tk, span=span),
        grid=(b, B_HEADS // 2),
        in_specs=[
            pl.BlockSpec((1, s, 2 * B_QK_PAD), rows),
            pl.BlockSpec((1, s, 2 * B_QK_PAD), rows),
            pl.BlockSpec((1, 2 * B_V, s), lambda bb, hp: (bb, hp, 0)),
            pl.BlockSpec((1, s, 2 * B_V), rows),
        ],
        out_specs=pl.BlockSpec((1, s, 2 * B_V), rows),
        out_shape=jax.ShapeDtypeStruct((b, s, B_HEADS * B_V), BF16),
        scratch_shapes=[pltpu.VMEM((B_SCORE_SLOTS, 2, span * tk, qb), F32)],
        compiler_params=_params("parallel", "arbitrary"),
        name="b_attn",
    )(q, k, vt, gate)


def _out_proj_kernel(x_ref, y_ref, w_ref, fg_ref, o_ref, *, final_norm):
    xn = x_ref[...] + jnp.dot(y_ref[...], w_ref[...], preferred_element_type=F32)
    if final_norm:
        xn = _rms(xn, fg_ref[...])
    o_ref[...] = xn


def _out_proj(x2, y2, w_bf, fg, final_norm, tm=1024):
    t = x2.shape[0]
    tok = pl.BlockSpec((tm, D_MODEL), lambda i: (i, 0))
    return pl.pallas_call(
        functools.partial(_out_proj_kernel, final_norm=final_norm),
        grid=(t // tm,),
        in_specs=[tok, tok,
                  pl.BlockSpec((D_MODEL, D_MODEL), lambda i: (0, 0)),
                  pl.BlockSpec((1, D_MODEL), lambda i: (0, 0))],
        out_specs=tok,
        out_shape=jax.ShapeDtypeStruct((t, D_MODEL), F32),
        compiler_params=_params("parallel"),
        name="out_proj",
    )(x2, y2, w_bf, fg)


def _mla_layer(x, o_prev, gate_prev, w_out_prev, g, wts, coeffs, fg, final_norm):
    b, s, _ = x.shape
    w1, qg, wq, kvg, wk, wvt, w_out = wts
    x, q, k, vt, gate = _b_proj(x, o_prev, gate_prev, w_out_prev, g, w1, qg, wq, kvg, wk, wvt,
                                coeffs)
    y = _mla_attn(q, k, vt, gate)
    out = _out_proj(x.reshape(b * s, D_MODEL), y.reshape(b * s, D_MODEL), w_out, fg, final_norm)
    return out.reshape(b, s, D_MODEL)


def _prep_mla_weights(w_in, q_g, w_q_up, kv_g, w_kv_up, w_out):
    c0, c1, c2 = B_Q_LORA, B_Q_LORA + B_KV_LORA, B_Q_LORA + B_KV_LORA + B_ROPE
    pad_r = B_QK_PAD - B_NOPE - B_ROPE
    w_kr = jnp.pad(w_in[:, c1:c2], ((0, 0), (B_NOPE, pad_r)))
    w1 = jnp.concatenate([w_in[:, :c1], w_kr, w_in[:, c2:]], axis=1).astype(BF16)
    wq = w_q_up.reshape(B_Q_LORA, B_HEADS, B_NOPE + B_ROPE)
    wq = jnp.pad(wq, ((0, 0), (0, 0), (0, pad_r))).reshape(B_Q_LORA, B_HEADS * B_QK_PAD)
    wkv = w_kv_up.reshape(B_KV_LORA, B_HEADS, B_NOPE + B_V)
    wk = jnp.pad(wkv[:, :, :B_NOPE], ((0, 0), (0, 0), (0, B_QK_PAD - B_NOPE)))
    wk = wk.reshape(B_KV_LORA, B_HEADS * B_QK_PAD)
    wvt = wkv[:, :, B_NOPE:].reshape(B_KV_LORA, B_HEADS * B_V).T
    return (w1, q_g.reshape(1, -1), wq.astype(BF16), kv_g.reshape(1, -1),
            wk.astype(BF16), wvt.astype(BF16), w_out.astype(BF16))


def _trunk(x, norm_g, a_wts, b_wts, coeffs, fg):
    for i in range(DEPTH):
        g = norm_g[i].reshape(1, D_MODEL)
        j = i // 2
        if i % 2 == 0:
            w_in_a, w_out_a = a_wts[j]
            o_a, gate_a = _dilated_layer(x, g, w_in_a)
        else:
            x = _mla_layer(x, o_a, gate_a, w_out_a, g, b_wts[j], coeffs, fg,
                           final_norm=(i == DEPTH - 1))
    return x


def kernel(x_prompt, x_sample, norm_g, a_w_in, a_w_out, b_w_in, b_q_norm_g, b_w_q_up,
           b_kv_norm_g, b_w_kv_up, b_w_out, final_norm_g):
    a_wts = [(a_w_in[j].astype(BF16).reshape(D_MODEL, A_NQKV + 1, A_GROUP_WIDTH).transpose(1, 0, 2),
              a_w_out[j].astype(BF16)) for j in range(a_w_in.shape[0])]
    b_wts = [_prep_mla_weights(b_w_in[j], b_q_norm_g[j], b_w_q_up[j], b_kv_norm_g[j],
                               b_w_kv_up[j], b_w_out[j]) for j in range(b_w_in.shape[0])]
    fg = final_norm_g.reshape(1, D_MODEL)
    outs = []
    for x in (x_prompt, x_sample):
        coeffs = _rope_coeffs(x.shape[1])
        outs.append(_trunk(x, norm_g, a_wts, b_wts, coeffs, fg))
    return tuple(outs)
```
